```python
import functools
import jax, jax.numpy as jnp
from jax import lax
import numpy as np

D_MODEL = 2048
BATCH = 8
SEQ = 4096
DEPTH = 1
DEC_BATCH = 8
DEC_SEQ = 64
PAST_LEN = 4096

CHUNK = 64
D_HEAD = 128
H_FOX = 8
H_BAND = 8
FOX_WIDTH = H_FOX * D_HEAD
BAND_WIDTH = H_BAND * D_HEAD
MIX_WIDTH = FOX_WIDTH + BAND_WIDTH
N_BAND_PAST = 8
BAND_ROWS = N_BAND_PAST * CHUNK
MAX_REL = 128
D_FF = 4 * D_MODEL
Q_BLOCK = 128
ALPHA = (2.0 * DEPTH) ** 0.25
BETA = (8.0 * DEPTH) ** -0.25
LN_EPS = 1e-5
NEG = -1e30
IN_COLS = 3 * FOX_WIDTH + H_FOX + 3 * BAND_WIDTH
SPLIT_AT = np.cumsum([FOX_WIDTH, FOX_WIDTH, FOX_WIDTH, H_FOX, BAND_WIDTH, BAND_WIDTH]).tolist()

kernel_name = "fox_chunkband_hybrid_stream_step"


def _norm(x):
    xf = x.astype(jnp.float32)
    mu = jnp.mean(xf, axis=-1, keepdims=True)
    var = jnp.mean(jnp.square(xf - mu), axis=-1, keepdims=True)
    return (xf - mu) * lax.rsqrt(var + LN_EPS)


def layer_norm(x, g, b):
    y = _norm(x) * g.astype(jnp.float32) + b.astype(jnp.float32)
    return y.astype(x.dtype)


def split_proj(u, w_in, b_forget):
    B_, S_ = u.shape[:2]
    proj = jnp.einsum('bsd,de->bse', u, w_in)
    q_a, k_a, v_a, f_a, q_b, k_b, v_b = jnp.split(proj, SPLIT_AT, axis=-1)
    heads = lambda t, h: t.reshape(B_, S_, h, D_HEAD)
    logf = jax.nn.log_sigmoid((f_a + b_forget).astype(jnp.float32))
    return (heads(q_a, H_FOX), heads(k_a, H_FOX), heads(v_a, H_FOX), logf,
            heads(q_b, H_BAND), heads(k_b, H_BAND), heads(v_b, H_BAND))


def fox_attend(q, k, v, f_q, f_k, q_pos, k_pos):
    s = jnp.einsum('bqhd,bkhd->bhqk', q, k).astype(jnp.float32) * (D_HEAD ** -0.5)
    s = s + jnp.swapaxes(f_q, 1, 2)[..., :, None] - jnp.swapaxes(f_k, 1, 2)[..., None, :]
    s = jnp.where(k_pos[None, :] <= q_pos[:, None], s, NEG)
    p = jax.nn.softmax(s, axis=-1).astype(v.dtype)
    return jnp.einsum('bhqk,bkhd->bqhd', p, v)


def fox_prompt(q, k, v, logf):
    B_, S_ = q.shape[:2]
    F = jnp.cumsum(logf, axis=1)
    pos = jnp.arange(S_)
    nqb = S_ // Q_BLOCK
    qb = jnp.swapaxes(q.reshape(B_, nqb, Q_BLOCK, H_FOX, D_HEAD), 0, 1)
    fb = jnp.swapaxes(F.reshape(B_, nqb, Q_BLOCK, H_FOX), 0, 1)
    pb = pos.reshape(nqb, Q_BLOCK)
    out = lax.map(lambda a: fox_attend(a[0], k, v, a[1], F, a[2], pos), (qb, fb, pb))
    return jnp.swapaxes(out, 0, 1).reshape(B_, S_, FOX_WIDTH)


def fox_sample(q, k, v, logf, k_c, v_c, logf_c):
    B_, T = q.shape[:2]
    P = k_c.shape[1]
    F = jnp.cumsum(jnp.concatenate([logf_c.astype(jnp.float32), logf], axis=1), axis=1)
    k_all = jnp.concatenate([k_c.astype(k.dtype), k], axis=1)
    v_all = jnp.concatenate([v_c.astype(v.dtype), v], axis=1)
    pos = jnp.arange(P + T)
    out = fox_attend(q, k_all, v_all, F[:, P:], F, pos[P:], pos)
    return out.reshape(B_, T, FOX_WIDTH)


def rel_bias_matrix(table, rel):
    return table[:, jnp.clip(rel, -MAX_REL, MAX_REL) + MAX_REL].astype(jnp.float32)


def band_prompt(q, k, v, table):
    B_, S_ = q.shape[:2]
    L = CHUNK
    NC = S_ // L
    K_LEN = (N_BAND_PAST + 1) * L
    qc = q.reshape(B_, NC, L, H_BAND, D_HEAD)
    pad = ((0, 0), (N_BAND_PAST, 0), (0, 0), (0, 0), (0, 0))
    kp = jnp.pad(k.reshape(B_, NC, L, H_BAND, D_HEAD), pad)
    vp = jnp.pad(v.reshape(B_, NC, L, H_BAND, D_HEAD), pad)
    offs = list(range(N_BAND_PAST, -1, -1))
    s = jnp.concatenate(
        [jnp.einsum('bnqhd,bnkhd->bnhqk', qc, kp[:, N_BAND_PAST - o:N_BAND_PAST - o + NC]) for o in offs],
        axis=-1).astype(jnp.float32) * (D_HEAD ** -0.5)
    kk = jnp.arange(K_LEN)
    rel = kk[None, :] - N_BAND_PAST * L - jnp.arange(L)[:, None]
    s = s + rel_bias_matrix(table, rel)
    valid = (jnp.arange(NC)[:, None] * L + kk[None, :] - N_BAND_PAST * L) >= 0
    s = jnp.where(valid[:, None, None, :], s, NEG)
    p = jax.nn.softmax(s, axis=-1).astype(v.dtype)
    p_parts = jnp.split(p, N_BAND_PAST + 1, axis=-1)
    out = sum(jnp.einsum('bnhqk,bnkhd->bnqhd', pp, vp[:, N_BAND_PAST - o:N_BAND_PAST - o + NC])
              for pp, o in zip(p_parts, offs))
    return out.reshape(B_, S_, BAND_WIDTH)


def band_sample(q, k, v, k_c, v_c, table):
    B_, T = q.shape[:2]
    W = k_c.shape[1]
    k_all = jnp.concatenate([k_c.astype(k.dtype), k], axis=1)
    v_all = jnp.concatenate([v_c.astype(v.dtype), v], axis=1)
    s = jnp.einsum('bqhd,bkhd->bhqk', q, k_all).astype(jnp.float32) * (D_HEAD ** -0.5)
    rel = jnp.arange(W + T)[None, :] - W - jnp.arange(T)[:, None]
    s = s + rel_bias_matrix(table, rel)
    p = jax.nn.softmax(s, axis=-1).astype(v.dtype)
    return jnp.einsum('bhqk,bkhd->bqhd', p, v_all).reshape(B_, T, BAND_WIDTH)


def prompt_mixer(table, q_a, k_a, v_a, logf, q_b, k_b, v_b):
    W = min(BAND_ROWS, q_b.shape[1])
    mix = jnp.concatenate([fox_prompt(q_a, k_a, v_a, logf), band_prompt(q_b, k_b, v_b, table)], axis=-1)
    return mix, (k_a, v_a, logf, k_b[:, -W:], v_b[:, -W:])


def sample_mixer(table, fk_c, fv_c, flogf_c, bk_c, bv_c, q_a, k_a, v_a, logf, q_b, k_b, v_b):
    mix = jnp.concatenate([fox_sample(q_a, k_a, v_a, logf, fk_c, fv_c, flogf_c),
                           band_sample(q_b, k_b, v_b, bk_c, bv_c, table)], axis=-1)
    return mix, (k_a, v_a, logf, k_b, v_b)


def trunk_layer(x, c, mixer, w_ada, b_ada, w_in, b_forget, w_out, ln_mix_g, ln_mix_b,
                w_up, b_up, w_down, ln_mlp_g, ln_mlp_b):
    ada = jnp.einsum('bd,de->be', jax.nn.silu(c), w_ada) + b_ada
    sh_a, sc_a, g_a, sh_m, sc_m, g_m = [t[:, None, :] for t in jnp.split(ada, 6, axis=-1)]
    u = (_norm(x) * (1.0 + sc_a) + sh_a).astype(x.dtype)
    mix, states = mixer(*split_proj(u, w_in, b_forget))
    x = layer_norm(ALPHA * x + g_a * jnp.einsum('bse,ed->bsd', mix, w_out), ln_mix_g, ln_mix_b)
    u = (_norm(x) * (1.0 + sc_m) + sh_m).astype(x.dtype)
    h = jnp.square(jax.nn.relu(jnp.einsum('bsd,df->bsf', u, w_up) + b_up))
    x = layer_norm(ALPHA * x + g_m * jnp.einsum('bsf,fd->bsd', h, w_down), ln_mlp_g, ln_mlp_b)
    return x, states


def setup_inputs(seed: int = 0) -> dict:
    key = jax.random.key(seed)
    ks = jax.random.split(key, 24)
    f32 = jnp.float32
    nrm = lambda k, shape, s=1.0: jax.random.normal(k, shape, f32) * s
    W = min(BAND_ROWS, PAST_LEN)
    return {
        "x_prompt": nrm(ks[0], (BATCH, SEQ, D_MODEL)),
        "x_sample": nrm(ks[1], (DEC_BATCH, DEC_SEQ, D_MODEL)),
        "cache_fox_k": nrm(ks[2], (DEPTH, DEC_BATCH, PAST_LEN, H_FOX, D_HEAD)),
        "cache_fox_v": nrm(ks[3], (DEPTH, DEC_BATCH, PAST_LEN, H_FOX, D_HEAD)),
        "cache_fox_logf": jax.nn.log_sigmoid(nrm(ks[4], (DEPTH, DEC_BATCH, PAST_LEN, H_FOX)) + 3.0),
        "cache_band_k": nrm(ks[5], (DEPTH, DEC_BATCH, W, H_BAND, D_HEAD)),
        "cache_band_v": nrm(ks[6], (DEPTH, DEC_BATCH, W, H_BAND, D_HEAD)),
        "c_prompt": nrm(ks[7], (BATCH, D_MODEL)),
        "c_sample": nrm(ks[8], (DEC_BATCH, D_MODEL)),
        "w_ada": nrm(ks[9], (DEPTH, D_MODEL, 6 * D_MODEL), 0.5 * D_MODEL ** -0.5),
        "b_ada": nrm(ks[10], (DEPTH, 6 * D_MODEL), 0.01),
        "w_in": nrm(ks[11], (DEPTH, D_MODEL, IN_COLS), D_MODEL ** -0.5),
        "b_forget": jnp.linspace(1.0, 5.0, H_FOX, dtype=f32)[None, :] + nrm(ks[12], (DEPTH, H_FOX), 0.1),
        "rel_bias": nrm(ks[13], (DEPTH, H_BAND, 2 * MAX_REL + 1), 0.2),
        "w_out": nrm(ks[14], (DEPTH, MIX_WIDTH, D_MODEL), BETA * MIX_WIDTH ** -0.5),
        "ln_mix_g": 1.0 + nrm(ks[15], (DEPTH, D_MODEL), 0.02),
        "ln_mix_b": nrm(ks[16], (DEPTH, D_MODEL), 0.02),
        "w_up": nrm(ks[17], (DEPTH, D_MODEL, D_FF), D_MODEL ** -0.5),
        "b_up": nrm(ks[18], (DEPTH, D_FF), 0.01),
        "w_down": nrm(ks[19], (DEPTH, D_FF, D_MODEL), BETA * D_FF ** -0.5),
        "ln_mlp_g": 1.0 + nrm(ks[20], (DEPTH, D_MODEL), 0.02),
        "ln_mlp_b": nrm(ks[21], (DEPTH, D_MODEL), 0.02),
    }


def reference(x_prompt, x_sample, cache_fox_k, cache_fox_v, cache_fox_logf, cache_band_k, cache_band_v,
              c_prompt, c_sample, w_ada, b_ada, w_in, b_forget, rel_bias, w_out, ln_mix_g, ln_mix_b,
              w_up, b_up, w_down, ln_mlp_g, ln_mlp_b):
    xp, xs = x_prompt, x_sample
    p_states, s_states = [], []
    for l in range(DEPTH):
        wl = (w_ada[l], b_ada[l], w_in[l], b_forget[l], w_out[l], ln_mix_g[l], ln_mix_b[l],
              w_up[l], b_up[l], w_down[l], ln_mlp_g[l], ln_mlp_b[l])
        xp, sp = trunk_layer(xp, c_prompt, functools.partial(prompt_mixer, rel_bias[l]), *wl)
        xs, ss = trunk_layer(xs, c_sample,
                             functools.partial(sample_mixer, rel_bias[l], cache_fox_k[l], cache_fox_v[l],
                                               cache_fox_logf[l], cache_band_k[l], cache_band_v[l]), *wl)
        p_states.append(sp)
        s_states.append(ss)
    stk = lambda lst, i: jnp.stack([st[i] for st in lst], axis=0)
    return (xp, xs,
            stk(p_states, 0), stk(p_states, 1), stk(p_states, 2), stk(p_states, 3), stk(p_states, 4),
            stk(s_states, 0), stk(s_states, 1), stk(s_states, 2), stk(s_states, 3), stk(s_states, 4))
```

```python
import functools

import jax
import jax.numpy as jnp
from jax import lax
from jax.experimental import pallas as pl
from jax.experimental.pallas import tpu as pltpu

D_HEAD = 128
N_HEADS = 8
GROUP_WIDTH = N_HEADS * D_HEAD
CHUNK = 64
BAND_ROWS = 512
MAX_REL = 128
N_REL = 2 * MAX_REL + 1
ALPHA = 2.0 ** 0.25
LN_EPS = 1e-5
NEG = -1e30
ATTN_SCALE = D_HEAD ** -0.5

VMEM_LIMIT_BYTES = 60 * 1024 * 1024

F32 = jnp.float32
BF16 = jnp.bfloat16


def _params(n_axes, vmem=None):
    return pltpu.CompilerParams(dimension_semantics=("arbitrary",) * n_axes, vmem_limit_bytes=vmem)


def _norm(x):
    mu = jnp.mean(x, axis=-1, keepdims=True)
    xc = x - mu
    var = jnp.mean(xc * xc, axis=-1, keepdims=True)
    return xc * lax.rsqrt(var + LN_EPS)


def _dot_nt(a, b):
    return lax.dot_general(a, b, (((1,), (1,)), ((), ())), preferred_element_type=F32)


def _dot(a, b):
    return jnp.dot(a, b, preferred_element_type=F32)


ROW_CHUNK = 256


def _for_row_chunks(bb, t, fn):
    rows = min(t, ROW_CHUNK)
    for b in range(bb):
        if t == rows:
            fn(b, 0, rows)
        else:
            def body(c, carry, b=b):
                fn(b, pl.multiple_of(c * rows, rows), rows)
                return carry
            lax.fori_loop(0, t // rows, body, 0)


def _ada_kernel(c_ref, w_ref, b_ref, o_ref):
    a = jax.nn.silu(c_ref[...]).astype(BF16)
    o_ref[...] = _dot(a, w_ref[...].astype(BF16)) + b_ref[...]


def _ada(c, w_ada, b_ada, tn=1024):
    n_rows, d = c.shape
    n_out = w_ada.shape[1]
    return pl.pallas_call(
        _ada_kernel,
        grid=(n_out // tn,),
        in_specs=[pl.BlockSpec((n_rows, d), lambda j: (0, 0)),
                  pl.BlockSpec((d, tn), lambda j: (0, j)),
                  pl.BlockSpec((1, tn), lambda j: (0, j))],
        out_specs=pl.BlockSpec((n_rows, tn), lambda j: (0, j)),
        out_shape=jax.ShapeDtypeStruct((n_rows, n_out), F32),
        compiler_params=_params(1, 40 * 1024 * 1024),
        name="ada",
    )(c, w_ada, b_ada.reshape(1, n_out))


def _band_bias_kernel(tab_ref, o_ref):
    h = pl.program_id(0)
    t = BAND_ROWS
    w = 2 * t
    kidx = lax.broadcasted_iota(jnp.int32, (1, w), 1)
    d = jnp.where(kidx < t, kidx, kidx - w)
    idx_cur = jnp.clip(d, -MAX_REL, MAX_REL) + MAX_REL
    idx_prev = jnp.clip(d - t, -MAX_REL, MAX_REL) + MAX_REL

    def fill(r, rows):
        rc, rp = rows
        val = tab_ref[h, r]
        return jnp.where(idx_cur == r, val, rc), jnp.where(idx_prev == r, val, rp)

    rc, rp = lax.fori_loop(0, N_REL, fill, (jnp.zeros((1, w), F32), jnp.zeros((1, w), F32)))
    qrow = lax.broadcasted_iota(jnp.int32, (t, w), 0)

    def toeplitz(row):
        x = jnp.broadcast_to(row, (t, w))
        for bit in range(t.bit_length() - 1):
            x = jnp.where(((qrow >> bit) & 1) == 1, pltpu.roll(x, 1 << bit, 1), x)
        return x[:, :t]

    qc = lax.broadcasted_iota(jnp.int32, (t, t), 0) // CHUNK
    kc = lax.broadcasted_iota(jnp.int32, (t, t), 1) // CHUNK
    o_ref[0, 1] = jnp.where(kc <= qc, toeplitz(rc), NEG)
    o_ref[0, 0] = jnp.where(kc >= qc, toeplitz(rp), NEG)


def _band_bias(table):
    t = BAND_ROWS
    return pl.pallas_call(
        _band_bias_kernel,
        grid=(N_HEADS,),
        in_specs=[pl.BlockSpec(memory_space=pltpu.SMEM)],
        out_specs=pl.BlockSpec((1, 2, t, t), lambda h: (h, 0, 0, 0)),
        out_shape=jax.ShapeDtypeStruct((N_HEADS, 2, t, t), F32),
        compiler_params=_params(1, 40 * 1024 * 1024),
        name="band_bias",
    )(table)


def _in_proj_kernel(x_ref, ada_ref, w_ref, wf_ref, bf_ref,
                    qa_ref, ka_ref, va_ref, qb_ref, kb_ref, vb_ref, lf_ref, u_ref, *, tiles_per_out):
    j = pl.program_id(2)
    bb, t, d = x_ref.shape

    @pl.when(j == 0)
    def _():
        def modulate(b, start, rows):
            u = _norm(x_ref[b, pl.ds(start, rows), :]) * (1.0 + ada_ref[b, 1:2, :]) + ada_ref[b, 0:1, :]
            u_ref[pl.ds(b * t + start, rows), :] = u.astype(BF16)

        _for_row_chunks(bb, t, modulate)
        f = _dot_nt(wf_ref[...], u_ref[...])
        lf_ref[0] = jax.nn.log_sigmoid(f[:N_HEADS] + bf_ref[...])

    acc = _dot(u_ref[...], w_ref[...])
    tn = acc.shape[1]
    outs = ((qa_ref, ATTN_SCALE), (ka_ref, None), (va_ref, None),
            (qb_ref, ATTN_SCALE), (kb_ref, None), (vb_ref, None))
    for o, (ref, scale) in enumerate(outs):
        @pl.when(j // tiles_per_out == o)
        def _(ref=ref, scale=scale):
            val = acc if scale is None else acc * scale
            ref[...] = val.astype(ref.dtype).reshape(bb, t, tn)


def _in_proj(x, ada, w_main, wf_t, b_forget, *, bb, t, tn=512):
    n_b, s, d = x.shape
    m = bb * t
    tiles_per_out = GROUP_WIDTH // tn
    n_j = 6 * tiles_per_out

    def out_spec(o):
        return pl.BlockSpec(
            (bb, t, tn),
            lambda b, i, j, o=o: (b, i, jnp.clip(j - o * tiles_per_out, 0, tiles_per_out - 1)))

    tok = lambda dt: jax.ShapeDtypeStruct((n_b, s, GROUP_WIDTH), dt)
    return pl.pallas_call(
        functools.partial(_in_proj_kernel, tiles_per_out=tiles_per_out),
        grid=(n_b // bb, s // t, n_j),
        in_specs=[pl.BlockSpec((bb, t, d), lambda b, i, j: (b, i, 0)),
                  pl.BlockSpec((bb, 6, d), lambda b, i, j: (b, 0, 0)),
                  pl.BlockSpec((d, tn), lambda b, i, j: (0, j)),
                  pl.BlockSpec(wf_t.shape, lambda b, i, j: (0, 0)),
                  pl.BlockSpec((N_HEADS, 1), lambda b, i, j: (0, 0))],
        out_specs=[out_spec(o) for o in range(6)]
        + [pl.BlockSpec((1, N_HEADS, m), lambda b, i, j: (b, 0, i))],
        out_shape=[tok(BF16), tok(F32), tok(F32), tok(BF16), tok(F32), tok(F32),
                   jax.ShapeDtypeStruct((n_b // bb, N_HEADS, (s // t) * m), F32)],
        scratch_shapes=[pltpu.VMEM((m, d), BF16)],
        compiler_params=_params(3, VMEM_LIMIT_BYTES),
        name="in_proj",
    )(x, ada, w_main, wf_t, b_forget.reshape(N_HEADS, 1))


def _cumsum_kernel(x_ref, o_ref):
    x = x_ref[0]
    n = x.shape[1]
    lane = lax.broadcasted_iota(jnp.int32, x.shape, 1)
    shift = 1
    while shift < n:
        x = x + jnp.where(lane >= shift, pltpu.roll(x, shift, 1), 0.0)
        shift *= 2
    o_ref[0] = x


def _cumsum_lanes(x):
    n_b, h, n = x.shape
    spec = pl.BlockSpec((1, h, n), lambda b: (b, 0, 0))
    return pl.pallas_call(
        _cumsum_kernel, grid=(n_b,), in_specs=[spec], out_specs=spec,
        out_shape=jax.ShapeDtypeStruct(x.shape, F32),
        compiler_params=_params(1), name="cumsum",
    )(x)


def _pick_head(fcol, h):
    lane = lax.broadcasted_iota(jnp.int32, fcol.shape, 1)
    return jnp.sum(jnp.where(lane == h, fcol, 0.0), axis=1, keepdims=True)


def _fox_prompt_kernel(q_ref, k_ref, v_ref, frow_ref, fcol_ref, o_ref, kb_ref, vb_ref, *, blk):
    h = pl.program_id(1)
    qi = pl.program_id(2)

    @pl.when(qi == 0)
    def _():
        kb_ref[...] = k_ref[0].astype(BF16)
        vb_ref[...] = v_ref[0].astype(BF16)

    q = q_ref[0]
    fq = _pick_head(fcol_ref[0], h)

    def step(j, carry, masked):
        m, l, acc = carry
        start = pl.multiple_of(j * blk, blk)
        kj = kb_ref[pl.ds(start, blk), :]
        vj = vb_ref[pl.ds(start, blk), :]
        s = _dot_nt(q, kj) + fq - frow_ref[0, j]
        if masked:
            row = lax.broadcasted_iota(jnp.int32, s.shape, 0)
            col = lax.broadcasted_iota(jnp.int32, s.shape, 1)
            s = jnp.where(col <= row, s, NEG)
        m_new = jnp.maximum(m, jnp.max(s, axis=1, keepdims=True))
        p = jnp.exp(s - m_new)
        a = jnp.exp(m - m_new)
        l = a * l + jnp.sum(p, axis=1, keepdims=True)
        acc = a * acc + _dot(p.astype(BF16), vj)
        return m_new, l, acc

    init = (jnp.full((blk, 1), NEG, F32), jnp.zeros((blk, 1), F32), jnp.zeros((blk, D_HEAD), F32))
    carry = lax.fori_loop(0, qi, functools.partial(step, masked=False), init)
    _, l, acc = step(qi, carry, True)
    o_ref[0] = (acc / l).astype(o_ref.dtype)


def _fox_prompt(q, k, v, f, *, blk=512):
    n_b, s, _ = q.shape
    n_blk = s // blk
    frow = f.reshape(n_b * N_HEADS, n_blk, 1, blk)
    fcol = jnp.swapaxes(f, 1, 2)
    kv_spec = pl.BlockSpec((1, s, D_HEAD), lambda b, h, i: (b, 0, h))
    return pl.pallas_call(
        functools.partial(_fox_prompt_kernel, blk=blk),
        grid=(n_b, N_HEADS, n_blk),
        in_specs=[pl.BlockSpec((1, blk, D_HEAD), lambda b, h, i: (b, i, h)),
                  kv_spec, kv_spec,
                  pl.BlockSpec((1, n_blk, 1, blk), lambda b, h, i: (b * N_HEADS + h, 0, 0, 0)),
                  pl.BlockSpec((1, blk, N_HEADS), lambda b, h, i: (b, i, 0))],
        out_specs=pl.BlockSpec((1, blk, D_HEAD), lambda b, h, i: (b, i, h)),
        out_shape=jax.ShapeDtypeStruct((n_b, s, GROUP_WIDTH), BF16),
        scratch_shapes=[pltpu.VMEM((s, D_HEAD), BF16), pltpu.VMEM((s, D_HEAD), BF16)],
        compiler_params=_params(3, 48 * 1024 * 1024),
        name="fox_prompt",
    )(q, k, v, frow, fcol)


def _band_prompt_kernel(q_ref, k_ref, v_ref, bias_ref, o_ref, kb_ref, vb_ref):
    i = pl.program_id(2)
    t = BAND_ROWS

    @pl.when(i == 0)
    def _():
        kb_ref[...] = k_ref[0].astype(BF16)
        vb_ref[...] = v_ref[0].astype(BF16)

    q = q_ref[0]
    cur = pl.multiple_of(i * t, t)
    prev = pl.multiple_of(jnp.maximum(i - 1, 0) * t, t)
    s_c = _dot_nt(q, kb_ref[pl.ds(cur, t), :]) + bias_ref[0, 1]
    s_p = _dot_nt(q, kb_ref[pl.ds(prev, t), :]) + bias_ref[0, 0]
    s_p = jnp.where(i > 0, s_p, NEG)
    m = jnp.maximum(jnp.max(s_c, axis=1, keepdims=True), jnp.max(s_p, axis=1, keepdims=True))
    p_c = jnp.exp(s_c - m)
    p_p = jnp.exp(s_p - m)
    l = jnp.sum(p_c, axis=1, keepdims=True) + jnp.sum(p_p, axis=1, keepdims=True)
    acc = _dot(p_c.astype(BF16), vb_ref[pl.ds(cur, t), :]) + _dot(p_p.astype(BF16), vb_ref[pl.ds(prev, t), :])
    o_ref[0] = (acc / l).astype(o_ref.dtype)


def _band_prompt(q, k, v, bias):
    n_b, s, _ = q.shape
    t = BAND_ROWS
    kv_spec = pl.BlockSpec((1, s, D_HEAD), lambda h, b, i: (b, 0, h))
    return pl.pallas_call(
        _band_prompt_kernel,
        grid=(N_HEADS, n_b, s // t),
        in_specs=[pl.BlockSpec((1, t, D_HEAD), lambda h, b, i: (b, i, h)),
                  kv_spec, kv_spec,
                  pl.BlockSpec((1, 2, t, t), lambda h, b, i: (h, 0, 0, 0))],
        out_specs=pl.BlockSpec((1, t, D_HEAD), lambda h, b, i: (b, i, h)),
        out_shape=jax.ShapeDtypeStruct((n_b, s, GROUP_WIDTH), BF16),
        scratch_shapes=[pltpu.VMEM((s, D_HEAD), BF16), pltpu.VMEM((s, D_HEAD), BF16)],
        compiler_params=_params(3, 48 * 1024 * 1024),
        name="band_prompt",
    )(q, k, v, bias)


def _fox_sample_kernel(q_ref, kc_ref, vc_ref, kn_ref, vn_ref, frow_ref, fq_ref, o_ref, *, past):
    h = pl.program_id(1)
    q = q_ref[0]
    n_new = q.shape[0]
    fq = _pick_head(fq_ref[0], h)
    f_all = frow_ref[0]
    s_c = _dot_nt(q, kc_ref[0].astype(BF16)) + fq - f_all[:, :past]
    s_n = _dot_nt(q, kn_ref[0].astype(BF16)) + fq - f_all[:, past:past + n_new]
    row = lax.broadcasted_iota(jnp.int32, s_n.shape, 0)
    col = lax.broadcasted_iota(jnp.int32, s_n.shape, 1)
    s_n = jnp.where(col <= row, s_n, NEG)
    m = jnp.maximum(jnp.max(s_c, axis=1, keepdims=True), jnp.max(s_n, axis=1, keepdims=True))
    p_c = jnp.exp(s_c - m)
    p_n = jnp.exp(s_n - m)
    l = jnp.sum(p_c, axis=1, keepdims=True) + jnp.sum(p_n, axis=1, keepdims=True)
    acc = _dot(p_c.astype(BF16), vc_ref[0].astype(BF16)) + _dot(p_n.astype(BF16), vn_ref[0].astype(BF16))
    o_ref[0] = (acc / l).astype(o_ref.dtype)


def _fox_sample(q, k_cache, v_cache, k_new, v_new, f_all, f_q):
    n_b, n_new, _ = q.shape
    past = k_cache.shape[1]
    padded = f_all.shape[-1]
    new_spec = pl.BlockSpec((1, n_new, D_HEAD), lambda b, h: (b, 0, h))
    cache_spec = pl.BlockSpec((1, past, D_HEAD), lambda b, h: (b, 0, h))
    return pl.pallas_call(
        functools.partial(_fox_sample_kernel, past=past),
        grid=(n_b, N_HEADS),
        in_specs=[new_spec, cache_spec, cache_spec, new_spec, new_spec,
                  pl.BlockSpec((1, 1, padded), lambda b, h: (b * N_HEADS + h, 0, 0)),
                  pl.BlockSpec((1, n_new, N_HEADS), lambda b, h: (b, 0, 0))],
        out_specs=new_spec,
        out_shape=jax.ShapeDtypeStruct((n_b, n_new, GROUP_WIDTH), BF16),
        compiler_params=_params(2, 40 * 1024 * 1024),
        name="fox_sample",
    )(q, k_cache, v_cache, k_new, v_new, f_all.reshape(n_b * N_HEADS, 1, padded), f_q)


def _band_sample_kernel(q_ref, kc_ref, vc_ref, kn_ref, vn_ref, bias_ref, o_ref):
    q = q_ref[0]
    n_new = q.shape[0]
    s_c = _dot_nt(q, kc_ref[0].astype(BF16)) + bias_ref[0, 0]
    s_n = _dot_nt(q, kn_ref[0].astype(BF16)) + bias_ref[0, 1][:, :n_new]
    m = jnp.maximum(jnp.max(s_c, axis=1, keepdims=True), jnp.max(s_n, axis=1, keepdims=True))
    p_c = jnp.exp(s_c - m)
    p_n = jnp.exp(s_n - m)
    l = jnp.sum(p_c, axis=1, keepdims=True) + jnp.sum(p_n, axis=1, keepdims=True)
    acc = _dot(p_c.astype(BF16), vc_ref[0].astype(BF16)) + _dot(p_n.astype(BF16), vn_ref[0].astype(BF16))
    o_ref[0] = (acc / l).astype(o_ref.dtype)


def _band_sample(q, k_cache, v_cache, k_new, v_new, bias):
    n_b, n_new, _ = q.shape
    past = k_cache.shape[1]
    new_spec = pl.BlockSpec((1, n_new, D_HEAD), lambda h, b: (b, 0, h))
    cache_spec = pl.BlockSpec((1, past, D_HEAD), lambda h, b: (b, 0, h))
    return pl.pallas_call(
        _band_sample_kernel,
        grid=(N_HEADS, n_b),
        in_specs=[new_spec, cache_spec, cache_spec, new_spec, new_spec,
                  pl.BlockSpec((1, 2, n_new, BAND_ROWS), lambda h, b: (h, 0, 0, 0))],
        out_specs=new_spec,
        out_shape=jax.ShapeDtypeStruct((n_b, n_new, GROUP_WIDTH), BF16),
        compiler_params=_params(2),
        name="band_sample",
    )(q, k_cache, v_cache, k_new, v_new, bias)


def _out_proj_kernel(fox_ref, band_ref, x_ref, ada_ref, w_ref, g_ref, b_ref, o_ref):
    bb, t, d = x_ref.shape
    gw = fox_ref.shape[2]
    fox = fox_ref[...].reshape(bb * t, gw)
    band = band_ref[...].reshape(bb * t, gw)
    y = _dot(fox, w_ref[:gw, :]) + _dot(band, w_ref[gw:, :])
    o_ref[...] = y.reshape(bb, t, d)

    def post_ln(b, start, rows):
        sl = pl.ds(start, rows)
        z = ALPHA * x_ref[b, sl, :] + ada_ref[b, 2:3, :] * o_ref[b, sl, :]
        o_ref[b, sl, :] = _norm(z) * g_ref[...] + b_ref[...]

    _for_row_chunks(bb, t, post_ln)


def _out_proj(fox, band, x, ada, w_out, g, b, *, bb, t):
    n_b, s, d = x.shape
    tok = lambda w: pl.BlockSpec((bb, t, w), lambda bi, i: (bi, i, 0))
    vec = pl.BlockSpec((1, d), lambda bi, i: (0, 0))
    return pl.pallas_call(
        _out_proj_kernel,
        grid=(n_b // bb, s // t),
        in_specs=[tok(GROUP_WIDTH), tok(GROUP_WIDTH), tok(d),
                  pl.BlockSpec((bb, 6, d), lambda bi, i: (bi, 0, 0)),
                  pl.BlockSpec(w_out.shape, lambda bi, i: (0, 0)), vec, vec],
        out_specs=tok(d),
        out_shape=jax.ShapeDtypeStruct(x.shape, F32),
        compiler_params=_params(2, VMEM_LIMIT_BYTES),
        name="out_proj",
    )(fox, band, x, ada, w_out, g.reshape(1, d), b.reshape(1, d))


def _mlp_kernel(x_ref, ada_ref, wu_ref, bu_ref, wd_ref, g_ref, b_ref, o_ref, u_ref):
    f = pl.program_id(2)
    bb, t, d = x_ref.shape

    @pl.when(f == 0)
    def _():
        def modulate(b, start, rows):
            u = _norm(x_ref[b, pl.ds(start, rows), :]) * (1.0 + ada_ref[b, 4:5, :]) + ada_ref[b, 3:4, :]
            u_ref[pl.ds(b * t + start, rows), :] = u.astype(BF16)

        _for_row_chunks(bb, t, modulate)
        o_ref[...] = jnp.zeros_like(o_ref)

    hid = jnp.maximum(_dot(u_ref[...], wu_ref[...]) + bu_ref[...], 0.0)
    hid = (hid * hid).astype(BF16)
    o_ref[...] += _dot(hid, wd_ref[...]).reshape(bb, t, d)

    @pl.when(f == pl.num_programs(2) - 1)
    def _():
        def post_ln(b, start, rows):
            sl = pl.ds(start, rows)
            z = ALPHA * x_ref[b, sl, :] + ada_ref[b, 5:6, :] * o_ref[b, sl, :]
            o_ref[b, sl, :] = _norm(z) * g_ref[...] + b_ref[...]

        _for_row_chunks(bb, t, post_ln)


def _mlp(x, ada, w_up, b_up, w_down, g, b, *, bb, t, tf=512):
    n_b, s, d = x.shape
    d_ff = w_up.shape[1]
    tok = pl.BlockSpec((bb, t, d), lambda bi, i, f: (bi, i, 0))
    vec = pl.BlockSpec((1, d), lambda bi, i, f: (0, 0))
    return pl.pallas_call(
        _mlp_kernel,
        grid=(n_b // bb, s // t, d_ff // tf),
        in_specs=[tok, pl.BlockSpec((bb, 6, d), lambda bi, i, f: (bi, 0, 0)),
                  pl.BlockSpec((d, tf), lambda bi, i, f: (0, f)),
                  pl.BlockSpec((1, tf), lambda bi, i, f: (0, f)),
                  pl.BlockSpec((tf, d), lambda bi, i, f: (f, 0)), vec, vec],
        out_specs=tok,
        out_shape=jax.ShapeDtypeStruct(x.shape, F32),
        scratch_shapes=[pltpu.VMEM((bb * t, d), BF16)],
        compiler_params=_params(3, VMEM_LIMIT_BYTES),
        name="mlp",
    )(x, ada, w_up, b_up.reshape(1, d_ff), w_down, g.reshape(1, d), b.reshape(1, d))


def kernel(x_prompt, x_sample, cache_fox_k, cache_fox_v, cache_fox_logf, cache_band_k, cache_band_v,
           c_prompt, c_sample, w_ada, b_ada, w_in, b_forget, rel_bias, w_out, ln_mix_g, ln_mix_b,
           w_up, b_up, w_down, ln_mlp_g, ln_mlp_b):
    assert w_in.shape[0] == 1, "single-layer step"
    n_b, seq, d = x_prompt.shape
    n_bs, n_new, _ = x_sample.shape
    past = cache_fox_k.shape[2]
    gw = GROUP_WIDTH

    w = w_in[0]
    w_main = jnp.concatenate([w[:, :3 * gw], w[:, 3 * gw + N_HEADS:]], axis=1).astype(BF16)
    wf_t = jnp.pad(w[:, 3 * gw:3 * gw + N_HEADS].T, ((0, 16 - N_HEADS), (0, 0))).astype(BF16)
    w_out_b = w_out[0].astype(BF16)
    w_up_b = w_up[0].astype(BF16)
    w_down_b = w_down[0].astype(BF16)

    ada = _ada(jnp.concatenate([c_prompt, c_sample], axis=0), w_ada[0], b_ada[0])
    ada = ada.reshape(n_b + n_bs, 6, d)
    ada_p, ada_s = ada[:n_b], ada[n_b:]
    bias = _band_bias(rel_bias[0])

    qa, ka, va, qb, kb, vb, lft = _in_proj(x_prompt, ada_p, w_main, wf_t, b_forget[0], bb=1, t=1024)
    f_p = _cumsum_lanes(lft)
    fox = _fox_prompt(qa, ka, va, f_p)
    band = _band_prompt(qb, kb, vb, bias)
    x1 = _out_proj(fox, band, x_prompt, ada_p, w_out_b, ln_mix_g[0], ln_mix_b[0], bb=1, t=512)
    y_p = _mlp(x1, ada_p, w_up_b, b_up[0], w_down_b, ln_mlp_g[0], ln_mlp_b[0], bb=1, t=1024)

    heads = lambda a: a.reshape(1, a.shape[0], a.shape[1], N_HEADS, D_HEAD)
    band_w = min(BAND_ROWS, seq)
    p_states = (heads(ka), heads(va), jnp.swapaxes(lft, 1, 2)[None],
                heads(kb[:, seq - band_w:]), heads(vb[:, seq - band_w:]))

    qa_s, ka_s, va_s, qb_s, kb_s, vb_s, lft_s = _in_proj(
        x_sample, ada_s, w_main, wf_t, b_forget[0], bb=n_bs, t=n_new)
    logf_s = jnp.transpose(lft_s.reshape(N_HEADS, n_bs, n_new), (1, 0, 2))
    total = past + n_new
    padded = -(-total // 128) * 128
    logf_all = jnp.concatenate(
        [jnp.swapaxes(cache_fox_logf[0], 1, 2), logf_s, jnp.zeros((n_bs, N_HEADS, padded - total), F32)], axis=2)
    f_s = _cumsum_lanes(logf_all)
    f_q = jnp.swapaxes(f_s[:, :, past:total], 1, 2)
    fox_s = _fox_sample(qa_s, cache_fox_k[0].reshape(n_bs, past, gw), cache_fox_v[0].reshape(n_bs, past, gw),
                        ka_s, va_s, f_s, f_q)
    band_past = cache_band_k.shape[2]
    band_s = _band_sample(qb_s, cache_band_k[0].reshape(n_bs, band_past, gw),
                          cache_band_v[0].reshape(n_bs, band_past, gw), kb_s, vb_s, bias)
    x1_s = _out_proj(fox_s, band_s, x_sample, ada_s, w_out_b, ln_mix_g[0], ln_mix_b[0], bb=n_bs, t=n_new)
    y_s = _mlp(x1_s, ada_s, w_up_b, b_up[0], w_down_b, ln_mlp_g[0], ln_mlp_b[0], bb=n_bs, t=n_new)
    s_states = (heads(ka_s), heads(va_s), jnp.swapaxes(logf_s, 1, 2)[None], heads(kb_s), heads(vb_s))

    return (y_p, y_s) + p_states + s_states
```

```python
import functools
import math

import jax
import jax.numpy as jnp
from jax import lax
from jax.experimental import pallas as pl
from jax.experimental.pallas import tpu as pltpu

D_HEAD = 128
N_HEADS = 8
GROUP_WIDTH = N_HEADS * D_HEAD
CHUNK = 64
BAND_ROWS = 512
MAX_REL = 128
N_REL = 2 * MAX_REL + 1
ALPHA = 2.0 ** 0.25
LN_EPS = 1e-5
NEG = -1e30
LOG2E = math.log2(math.e)
Q_SCALE = D_HEAD ** -0.5 * LOG2E

VMEM_LIMIT_BYTES = 60 * 1024 * 1024

F32 = jnp.float32
BF16 = jnp.bfloat16


def _params(n_axes, vmem=None):
    return pltpu.CompilerParams(dimension_semantics=("arbitrary",) * n_axes, vmem_limit_bytes=vmem)


def _norm(x):
    mu = jnp.mean(x, axis=-1, keepdims=True)
    xc = x - mu
    var = jnp.mean(xc * xc, axis=-1, keepdims=True)
    return xc * lax.rsqrt(var + LN_EPS)


def _dot_nt(a, b):
    return lax.dot_general(a, b, (((1,), (1,)), ((), ())), preferred_element_type=F32)


def _dot(a, b):
    return jnp.dot(a, b, preferred_element_type=F32)


ROW_CHUNK = 256


def _for_row_chunks(bb, t, fn):
    rows = min(t, ROW_CHUNK)
    for b in range(bb):
        if t == rows:
            fn(b, 0, rows)
        else:
            def body(c, carry, b=b):
                fn(b, pl.multiple_of(c * rows, rows), rows)
                return carry
            lax.fori_loop(0, t // rows, body, 0)


def _ada_kernel(c_ref, w_ref, b_ref, o_ref):
    a = jax.nn.silu(c_ref[...]).astype(BF16)
    o_ref[...] = _dot(a, w_ref[...].astype(BF16)) + b_ref[...]


def _ada(c, w_ada, b_ada, tn=1024):
    n_rows, d = c.shape
    n_out = w_ada.shape[1]
    return pl.pallas_call(
        _ada_kernel,
        grid=(n_out // tn,),
        in_specs=[pl.BlockSpec((n_rows, d), lambda j: (0, 0)),
                  pl.BlockSpec((d, tn), lambda j: (0, j)),
                  pl.BlockSpec((1, tn), lambda j: (0, j))],
        out_specs=pl.BlockSpec((n_rows, tn), lambda j: (0, j)),
        out_shape=jax.ShapeDtypeStruct((n_rows, n_out), F32),
        compiler_params=_params(1, 40 * 1024 * 1024),
        name="ada",
    )(c, w_ada, b_ada.reshape(1, n_out))


def _band_bias_kernel(tab_ref, o_ref):
    h = pl.program_id(0)
    t = BAND_ROWS
    w = 2 * t
    kidx = lax.broadcasted_iota(jnp.int32, (1, w), 1)
    d = jnp.where(kidx < t, kidx, kidx - w)
    idx_cur = jnp.clip(d, -MAX_REL, MAX_REL) + MAX_REL
    idx_prev = jnp.clip(d - t, -MAX_REL, MAX_REL) + MAX_REL

    def fill(r, rows):
        rc, rp = rows
        val = tab_ref[h, r] * LOG2E
        return jnp.where(idx_cur == r, val, rc), jnp.where(idx_prev == r, val, rp)

    rc, rp = lax.fori_loop(0, N_REL, fill, (jnp.zeros((1, w), F32), jnp.zeros((1, w), F32)))
    qrow = lax.broadcasted_iota(jnp.int32, (t, w), 0)

    def toeplitz(row):
        x = jnp.broadcast_to(row, (t, w))
        for bit in range(t.bit_length() - 1):
            x = jnp.where(((qrow >> bit) & 1) == 1, pltpu.roll(x, 1 << bit, 1), x)
        return x[:, :t]

    qc = lax.broadcasted_iota(jnp.int32, (t, t), 0) // CHUNK
    kc = lax.broadcasted_iota(jnp.int32, (t, t), 1) // CHUNK
    o_ref[0, 1] = jnp.where(kc <= qc, toeplitz(rc), NEG)
    o_ref[0, 0] = jnp.where(kc >= qc, toeplitz(rp), NEG)


def _band_bias(table):
    t = BAND_ROWS
    return pl.pallas_call(
        _band_bias_kernel,
        grid=(N_HEADS,),
        in_specs=[pl.BlockSpec(memory_space=pltpu.SMEM)],
        out_specs=pl.BlockSpec((1, 2, t, t), lambda h: (h, 0, 0, 0)),
        out_shape=jax.ShapeDtypeStruct((N_HEADS, 2, t, t), F32),
        compiler_params=_params(1, 40 * 1024 * 1024),
        name="band_bias",
    )(table)


def _in_proj_kernel(x_ref, ada_ref, w_ref, wf_ref, bf_ref,
                    qa_ref, ka_ref, va_ref, qb_ref, kb_ref, vb_ref, lf_ref, u_ref, *, tiles_per_out):
    j = pl.program_id(2)
    bb, t, d = x_ref.shape

    @pl.when(j == 0)
    def _():
        def modulate(b, start, rows):
            u = _norm(x_ref[b, pl.ds(start, rows), :]) * (1.0 + ada_ref[b, 1:2, :]) + ada_ref[b, 0:1, :]
            u_ref[pl.ds(b * t + start, rows), :] = u.astype(BF16)

        _for_row_chunks(bb, t, modulate)
        f = _dot_nt(wf_ref[...], u_ref[...])
        lf_ref[0] = jax.nn.log_sigmoid(f[:N_HEADS] + bf_ref[...])

    acc = _dot(u_ref[...], w_ref[...])
    tn = acc.shape[1]
    outs = ((qa_ref, Q_SCALE), (ka_ref, None), (va_ref, None),
            (qb_ref, Q_SCALE), (kb_ref, None), (vb_ref, None))
    for o, (ref, scale) in enumerate(outs):
        @pl.when(j // tiles_per_out == o)
        def _(ref=ref, scale=scale):
            val = acc if scale is None else acc * scale
            ref[...] = val.astype(ref.dtype).reshape(bb, t, tn)


def _in_proj(x, ada, w_main, wf_t, b_forget, *, bb, t, tn=512):
    n_b, s, d = x.shape
    m = bb * t
    tiles_per_out = GROUP_WIDTH // tn
    n_j = 6 * tiles_per_out

    def out_spec(o):
        return pl.BlockSpec(
            (bb, t, tn),
            lambda b, i, j, o=o: (b, i, jnp.clip(j - o * tiles_per_out, 0, tiles_per_out - 1)))

    tok = lambda dt: jax.ShapeDtypeStruct((n_b, s, GROUP_WIDTH), dt)
    return pl.pallas_call(
        functools.partial(_in_proj_kernel, tiles_per_out=tiles_per_out),
        grid=(n_b // bb, s // t, n_j),
        in_specs=[pl.BlockSpec((bb, t, d), lambda b, i, j: (b, i, 0)),
                  pl.BlockSpec((bb, 6, d), lambda b, i, j: (b, 0, 0)),
                  pl.BlockSpec((d, tn), lambda b, i, j: (0, j)),
                  pl.BlockSpec(wf_t.shape, lambda b, i, j: (0, 0)),
                  pl.BlockSpec((N_HEADS, 1), lambda b, i, j: (0, 0))],
        out_specs=[out_spec(o) for o in range(6)]
        + [pl.BlockSpec((1, N_HEADS, m), lambda b, i, j: (b, 0, i))],
        out_shape=[tok(BF16), tok(F32), tok(F32), tok(BF16), tok(F32), tok(F32),
                   jax.ShapeDtypeStruct((n_b // bb, N_HEADS, (s // t) * m), F32)],
        scratch_shapes=[pltpu.VMEM((m, d), BF16)],
        compiler_params=_params(3, VMEM_LIMIT_BYTES),
        name="in_proj",
    )(x, ada, w_main, wf_t, b_forget.reshape(N_HEADS, 1))


def _cumsum_kernel(x_ref, o_ref):
    x = x_ref[0]
    n = x.shape[1]
    lane = lax.broadcasted_iota(jnp.int32, x.shape, 1)
    shift = 1
    while shift < n:
        x = x + jnp.where(lane >= shift, pltpu.roll(x, shift, 1), 0.0)
        shift *= 2
    o_ref[0] = x * LOG2E


def _cumsum_lanes(x):
    n_b, h, n = x.shape
    spec = pl.BlockSpec((1, h, n), lambda b: (b, 0, 0))
    return pl.pallas_call(
        _cumsum_kernel, grid=(n_b,), in_specs=[spec], out_specs=spec,
        out_shape=jax.ShapeDtypeStruct(x.shape, F32),
        compiler_params=_params(1), name="cumsum",
    )(x)


def _pick_head(fcol, h):
    lane = lax.broadcasted_iota(jnp.int32, fcol.shape, 1)
    return jnp.sum(jnp.where(lane == h, fcol, 0.0), axis=1, keepdims=True)


def _fox_prompt_kernel(q_ref, k_ref, v_ref, frow_ref, fcol_ref, o_ref,
                       kb_ref, vb_ref, fq_ref, m_ref, l_ref, acc_ref, *, blk, strip, hp):
    hg = pl.program_id(1)
    qi = pl.program_id(2)
    lanes = D_HEAD
    n_col = blk // lanes
    head_lanes = [slice(hh * lanes, (hh + 1) * lanes) for hh in range(hp)]

    @pl.when(qi == 0)
    def _():
        for hh in range(hp):
            kb_ref[hh] = k_ref[0, :, head_lanes[hh]].astype(BF16)
            vb_ref[hh] = v_ref[0, :, head_lanes[hh]].astype(BF16)

    for hh in range(hp):
        fq_ref[hh] = jnp.broadcast_to(_pick_head(fcol_ref[0], hg * hp + hh), (blk, lanes))
    m_ref[...] = jnp.full(m_ref.shape, NEG, F32)
    l_ref[...] = jnp.zeros(l_ref.shape, F32)
    acc_ref[...] = jnp.zeros(acc_ref.shape, F32)

    def scores(hh, j):
        start = pl.multiple_of(j * blk, blk)
        return _dot_nt(q_ref[0, :, head_lanes[hh]], kb_ref[hh, pl.ds(start, blk), :])

    def softmax(hh, j, s_all, masked):
        f_keys = frow_ref[hh, j]
        p_strips = []
        for r in range(blk // strip):
            rows = slice(r * strip, (r + 1) * strip)
            fq = fq_ref[hh, rows]
            cols = []
            for cc in range(n_col):
                k0 = cc * lanes
                if masked and k0 >= (r + 1) * strip:
                    continue
                s = s_all[rows, k0:k0 + lanes] + fq - f_keys[:, k0:k0 + lanes]
                if masked and k0 + lanes - 1 > r * strip:
                    row = lax.broadcasted_iota(jnp.int32, s.shape, 0) + r * strip
                    col = lax.broadcasted_iota(jnp.int32, s.shape, 1) + k0
                    s = jnp.where(col <= row, s, NEG)
                cols.append(s)
            m_old = m_ref[hh, rows]
            m_blk = functools.reduce(jnp.maximum, cols)
            m_new = jnp.maximum(m_old, jnp.broadcast_to(jnp.max(m_blk, axis=1, keepdims=True), m_old.shape))
            ps = [jnp.exp2(s - m_new) for s in cols]
            a = jnp.exp2(m_old - m_new)
            l_ref[hh, rows] = a * l_ref[hh, rows] + functools.reduce(jnp.add, ps)
            m_ref[hh, rows] = m_new
            acc_ref[hh, rows] = a * acc_ref[hh, rows]
            ps = [p.astype(BF16) for p in ps] + [jnp.zeros((strip, lanes), BF16)] * (n_col - len(ps))
            p_strips.append(jnp.concatenate(ps, axis=1))
        return jnp.concatenate(p_strips, axis=0)

    def values(hh, j, p):
        start = pl.multiple_of(j * blk, blk)
        acc_ref[hh] += _dot(p, vb_ref[hh, pl.ds(start, blk), :])

    def block_step(j, masked):
        s_next = scores(0, j)
        for hh in range(hp):
            s_cur = s_next
            if hh + 1 < hp:
                s_next = scores(hh + 1, j)
            values(hh, j, softmax(hh, j, s_cur, masked))

    def loop_body(j, carry):
        block_step(j, False)
        return carry

    lax.fori_loop(0, qi, loop_body, 0)
    block_step(qi, True)
    for hh in range(hp):
        l = jnp.sum(l_ref[hh], axis=1, keepdims=True)
        o_ref[0, :, head_lanes[hh]] = (acc_ref[hh] / l).astype(o_ref.dtype)


def _fox_prompt(q, k, v, f, *, blk=512, strip=32, hp=4):
    n_b, s, _ = q.shape
    n_blk = s // blk
    n_hg = N_HEADS // hp
    frow = f.reshape(n_b * N_HEADS, n_blk, 1, blk)
    fcol = jnp.swapaxes(f, 1, 2)
    kv_spec = pl.BlockSpec((1, s, hp * D_HEAD), lambda b, h, i: (b, 0, h))
    q_spec = pl.BlockSpec((1, blk, hp * D_HEAD), lambda b, h, i: (b, i, h))
    state = pltpu.VMEM((hp, blk, D_HEAD), F32)
    return pl.pallas_call(
        functools.partial(_fox_prompt_kernel, blk=blk, strip=strip, hp=hp),
        grid=(n_b, n_hg, n_blk),
        in_specs=[q_spec, kv_spec, kv_spec,
                  pl.BlockSpec((hp, n_blk, 1, blk), lambda b, h, i: (b * n_hg + h, 0, 0, 0)),
                  pl.BlockSpec((1, blk, N_HEADS), lambda b, h, i: (b, i, 0))],
        out_specs=q_spec,
        out_shape=jax.ShapeDtypeStruct((n_b, s, GROUP_WIDTH), BF16),
        scratch_shapes=[pltpu.VMEM((hp, s, D_HEAD), BF16), pltpu.VMEM((hp, s, D_HEAD), BF16),
                        state, state, state, state],
        compiler_params=_params(3, VMEM_LIMIT_BYTES),
        name="fox_prompt",
    )(q, k, v, frow, fcol)


def _band_prompt_kernel(q_ref, k_ref, v_ref, bias_ref, o_ref, kb_ref, vb_ref, *, tb, strip, hp):
    i = pl.program_id(2)
    past = BAND_ROWS
    lanes = D_HEAD
    win = past + tb
    head_lanes = [slice(hh * lanes, (hh + 1) * lanes) for hh in range(hp)]

    @pl.when(i == 0)
    def _():
        for hh in range(hp):
            kb_ref[hh, :past, :] = jnp.zeros((past, D_HEAD), BF16)
            vb_ref[hh, :past, :] = jnp.zeros((past, D_HEAD), BF16)
            kb_ref[hh, past:, :] = k_ref[0, :, head_lanes[hh]].astype(BF16)
            vb_ref[hh, past:, :] = v_ref[0, :, head_lanes[hh]].astype(BF16)

    start = pl.multiple_of(i * tb, tb)
    first_valid = past - start

    def scores(hh):
        return _dot_nt(q_ref[0, :, head_lanes[hh]], kb_ref[hh, pl.ds(start, win), :])

    def softmax(hh, s_all):
        p_strips, l_strips = [], []
        for r in range(tb // strip):
            rows = slice(r * strip, (r + 1) * strip)
            q_chunk = (r * strip) // CHUNK
            cols = {}
            for cc in range(win // lanes):
                k0 = cc * lanes
                if k0 < past:
                    if (k0 + lanes - 1) // CHUNK < q_chunk:
                        continue
                    s = s_all[rows, k0:k0 + lanes] + bias_ref[hh, 0, rows, k0:k0 + lanes]
                    col = lax.broadcasted_iota(jnp.int32, s.shape, 1) + k0
                    s = jnp.where(col >= first_valid, s, NEG)
                else:
                    if (k0 - past) // CHUNK > q_chunk:
                        continue
                    s = s_all[rows, k0:k0 + lanes] + bias_ref[hh, 1, rows, k0 - past:k0 - past + lanes]
                cols[cc] = s
            m_blk = functools.reduce(jnp.maximum, cols.values())
            m = jnp.broadcast_to(jnp.max(m_blk, axis=1, keepdims=True), m_blk.shape)
            ps = {cc: jnp.exp2(s - m) for cc, s in cols.items()}
            l_strips.append(functools.reduce(jnp.add, ps.values()))
            zero = jnp.zeros((strip, lanes), BF16)
            p_strips.append(jnp.concatenate(
                [ps[cc].astype(BF16) if cc in ps else zero for cc in range(win // lanes)], axis=1))
        return jnp.concatenate(p_strips, axis=0), jnp.concatenate(l_strips, axis=0)

    def values(hh, p, l_part):
        acc = _dot(p, vb_ref[hh, pl.ds(start, win), :])
        l = jnp.sum(l_part, axis=1, keepdims=True)
        o_ref[0, :, head_lanes[hh]] = (acc / l).astype(o_ref.dtype)

    s_next = scores(0)
    for hh in range(hp):
        s_cur = s_next
        if hh + 1 < hp:
            s_next = scores(hh + 1)
        values(hh, *softmax(hh, s_cur))


def _band_prompt(q, k, v, bias, *, tb=256, strip=32, hp=4):
    n_b, s, _ = q.shape
    kv_spec = pl.BlockSpec((1, s, hp * D_HEAD), lambda h, b, i: (b, 0, h))
    q_spec = pl.BlockSpec((1, tb, hp * D_HEAD), lambda h, b, i: (b, i, h))
    kv_scratch = pltpu.VMEM((hp, BAND_ROWS + s, D_HEAD), BF16)
    return pl.pallas_call(
        functools.partial(_band_prompt_kernel, tb=tb, strip=strip, hp=hp),
        grid=(N_HEADS // hp, n_b, s // tb),
        in_specs=[q_spec, kv_spec, kv_spec,
                  pl.BlockSpec((hp, 2, tb, BAND_ROWS), lambda h, b, i: (h, 0, 0, 0))],
        out_specs=q_spec,
        out_shape=jax.ShapeDtypeStruct((n_b, s, GROUP_WIDTH), BF16),
        scratch_shapes=[kv_scratch, kv_scratch],
        compiler_params=_params(3, VMEM_LIMIT_BYTES),
        name="band_prompt",
    )(q, k, v, bias)


def _fox_sample_kernel(q_ref, kc_ref, vc_ref, kn_ref, vn_ref, frow_ref, fq_ref, o_ref, *, past):
    h = pl.program_id(1)
    q = q_ref[0]
    n_new = q.shape[0]
    fq = _pick_head(fq_ref[0], h)
    f_all = frow_ref[0]
    s_c = _dot_nt(q, kc_ref[0].astype(BF16)) + fq - f_all[:, :past]
    s_n = _dot_nt(q, kn_ref[0].astype(BF16)) + fq - f_all[:, past:past + n_new]
    row = lax.broadcasted_iota(jnp.int32, s_n.shape, 0)
    col = lax.broadcasted_iota(jnp.int32, s_n.shape, 1)
    s_n = jnp.where(col <= row, s_n, NEG)
    m = jnp.maximum(jnp.max(s_c, axis=1, keepdims=True), jnp.max(s_n, axis=1, keepdims=True))
    p_c = jnp.exp2(s_c - m)
    p_n = jnp.exp2(s_n - m)
    l = jnp.sum(p_c, axis=1, keepdims=True) + jnp.sum(p_n, axis=1, keepdims=True)
    acc = _dot(p_c.astype(BF16), vc_ref[0].astype(BF16)) + _dot(p_n.astype(BF16), vn_ref[0].astype(BF16))
    o_ref[0] = (acc / l).astype(o_ref.dtype)


def _fox_sample(q, k_cache, v_cache, k_new, v_new, f_all, f_q):
    n_b, n_new, _ = q.shape
    past = k_cache.shape[1]
    padded = f_all.shape[-1]
    new_spec = pl.BlockSpec((1, n_new, D_HEAD), lambda b, h: (b, 0, h))
    cache_spec = pl.BlockSpec((1, past, D_HEAD), lambda b, h: (b, 0, h))
    return pl.pallas_call(
        functools.partial(_fox_sample_kernel, past=past),
        grid=(n_b, N_HEADS),
        in_specs=[new_spec, cache_spec, cache_spec, new_spec, new_spec,
                  pl.BlockSpec((1, 1, padded), lambda b, h: (b * N_HEADS + h, 0, 0)),
                  pl.BlockSpec((1, n_new, N_HEADS), lambda b, h: (b, 0, 0))],
        out_specs=new_spec,
        out_shape=jax.ShapeDtypeStruct((n_b, n_new, GROUP_WIDTH), BF16),
        compiler_params=_params(2, 40 * 1024 * 1024),
        name="fox_sample",
    )(q, k_cache, v_cache, k_new, v_new, f_all.reshape(n_b * N_HEADS, 1, padded), f_q)


def _band_sample_kernel(q_ref, kc_ref, vc_ref, kn_ref, vn_ref, bias_ref, o_ref):
    q = q_ref[0]
    n_new = q.shape[0]
    s_c = _dot_nt(q, kc_ref[0].astype(BF16)) + bias_ref[0, 0]
    s_n = _dot_nt(q, kn_ref[0].astype(BF16)) + bias_ref[0, 1][:, :n_new]
    m = jnp.maximum(jnp.max(s_c, axis=1, keepdims=True), jnp.max(s_n, axis=1, keepdims=True))
    p_c = jnp.exp2(s_c - m)
    p_n = jnp.exp2(s_n - m)
    l = jnp.sum(p_c, axis=1, keepdims=True) + jnp.sum(p_n, axis=1, keepdims=True)
    acc = _dot(p_c.astype(BF16), vc_ref[0].astype(BF16)) + _dot(p_n.astype(BF16), vn_ref[0].astype(BF16))
    o_ref[0] = (acc / l).astype(o_ref.dtype)


def _band_sample(q, k_cache, v_cache, k_new, v_new, bias):
    n_b, n_new, _ = q.shape
    past = k_cache.shape[1]
    new_spec = pl.BlockSpec((1, n_new, D_HEAD), lambda h, b: (b, 0, h))
    cache_spec = pl.BlockSpec((1, past, D_HEAD), lambda h, b: (b, 0, h))
    return pl.pallas_call(
        _band_sample_kernel,
        grid=(N_HEADS, n_b),
        in_specs=[new_spec, cache_spec, cache_spec, new_spec, new_spec,
                  pl.BlockSpec((1, 2, n_new, BAND_ROWS), lambda h, b: (h, 0, 0, 0))],
        out_specs=new_spec,
        out_shape=jax.ShapeDtypeStruct((n_b, n_new, GROUP_WIDTH), BF16),
        compiler_params=_params(2),
        name="band_sample",
    )(q, k_cache, v_cache, k_new, v_new, bias)


def _out_proj_kernel(fox_ref, band_ref, x_ref, ada_ref, w_ref, g_ref, b_ref, o_ref):
    bb, t, d = x_ref.shape
    gw = fox_ref.shape[2]
    fox = fox_ref[...].reshape(bb * t, gw)
    band = band_ref[...].reshape(bb * t, gw)
    y = _dot(fox, w_ref[:gw, :]) + _dot(band, w_ref[gw:, :])
    o_ref[...] = y.reshape(bb, t, d)

    def post_ln(b, start, rows):
        sl = pl.ds(start, rows)
        z = ALPHA * x_ref[b, sl, :] + ada_ref[b, 2:3, :] * o_ref[b, sl, :]
        o_ref[b, sl, :] = _norm(z) * g_ref[...] + b_ref[...]

    _for_row_chunks(bb, t, post_ln)


def _out_proj(fox, band, x, ada, w_out, g, b, *, bb, t):
    n_b, s, d = x.shape
    tok = lambda w: pl.BlockSpec((bb, t, w), lambda bi, i: (bi, i, 0))
    vec = pl.BlockSpec((1, d), lambda bi, i: (0, 0))
    return pl.pallas_call(
        _out_proj_kernel,
        grid=(n_b // bb, s // t),
        in_specs=[tok(GROUP_WIDTH), tok(GROUP_WIDTH), tok(d),
                  pl.BlockSpec((bb, 6, d), lambda bi, i: (bi, 0, 0)),
                  pl.BlockSpec(w_out.shape, lambda bi, i: (0, 0)), vec, vec],
        out_specs=tok(d),
        out_shape=jax.ShapeDtypeStruct(x.shape, F32),
        compiler_params=_params(2, VMEM_LIMIT_BYTES),
        name="out_proj",
    )(fox, band, x, ada, w_out, g.reshape(1, d), b.reshape(1, d))


def _mlp_kernel(x_ref, ada_ref, wu_ref, bu_ref, wd_ref, g_ref, b_ref, o_ref, u_ref):
    f = pl.program_id(2)
    bb, t, d = x_ref.shape

    @pl.when(f == 0)
    def _():
        def modulate(b, start, rows):
            u = _norm(x_ref[b, pl.ds(start, rows), :]) * (1.0 + ada_ref[b, 4:5, :]) + ada_ref[b, 3:4, :]
            u_ref[pl.ds(b * t + start, rows), :] = u.astype(BF16)

        _for_row_chunks(bb, t, modulate)
        o_ref[...] = jnp.zeros_like(o_ref)

    hid = jnp.maximum(_dot(u_ref[...], wu_ref[...]) + bu_ref[...], 0.0)
    hid = (hid * hid).astype(BF16)
    o_ref[...] += _dot(hid, wd_ref[...]).reshape(bb, t, d)

    @pl.when(f == pl.num_programs(2) - 1)
    def _():
        def post_ln(b, start, rows):
            sl = pl.ds(start, rows)
            z = ALPHA * x_ref[b, sl, :] + ada_ref[b, 5:6, :] * o_ref[b, sl, :]
            o_ref[b, sl, :] = _norm(z) * g_ref[...] + b_ref[...]

        _for_row_chunks(bb, t, post_ln)


def _mlp(x, ada, w_up, b_up, w_down, g, b, *, bb, t, tf=512):
    n_b, s, d = x.shape
    d_ff = w_up.shape[1]
    tok = pl.BlockSpec((bb, t, d), lambda bi, i, f: (bi, i, 0))
    vec = pl.BlockSpec((1, d), lambda bi, i, f: (0, 0))
    return pl.pallas_call(
        _mlp_kernel,
        grid=(n_b // bb, s // t, d_ff // tf),
        in_specs=[tok, pl.BlockSpec((bb, 6, d), lambda bi, i, f: (bi, 0, 0)),
                  pl.BlockSpec((d, tf), lambda bi, i, f: (0, f)),
                  pl.BlockSpec((1, tf), lambda bi, i, f: (0, f)),
                  pl.BlockSpec((tf, d), lambda bi, i, f: (f, 0)), vec, vec],
        out_specs=tok,
        out_shape=jax.ShapeDtypeStruct(x.shape, F32),
        scratch_shapes=[pltpu.VMEM((bb * t, d), BF16)],
        compiler_params=_params(3, VMEM_LIMIT_BYTES),
        name="mlp",
    )(x, ada, w_up, b_up.reshape(1, d_ff), w_down, g.reshape(1, d), b.reshape(1, d))


def kernel(x_prompt, x_sample, cache_fox_k, cache_fox_v, cache_fox_logf, cache_band_k, cache_band_v,
           c_prompt, c_sample, w_ada, b_ada, w_in, b_forget, rel_bias, w_out, ln_mix_g, ln_mix_b,
           w_up, b_up, w_down, ln_mlp_g, ln_mlp_b):
    assert w_in.shape[0] == 1, "single-layer step"
    n_b, seq, d = x_prompt.shape
    n_bs, n_new, _ = x_sample.shape
    past = cache_fox_k.shape[2]
    gw = GROUP_WIDTH

    w = w_in[0]
    w_main = jnp.concatenate([w[:, :3 * gw], w[:, 3 * gw + N_HEADS:]], axis=1).astype(BF16)
    wf_t = jnp.pad(w[:, 3 * gw:3 * gw + N_HEADS].T, ((0, 16 - N_HEADS), (0, 0))).astype(BF16)
    w_out_b = w_out[0].astype(BF16)
    w_up_b = w_up[0].astype(BF16)
    w_down_b = w_down[0].astype(BF16)

    ada = _ada(jnp.concatenate([c_prompt, c_sample], axis=0), w_ada[0], b_ada[0])
    ada = ada.reshape(n_b + n_bs, 6, d)
    ada_p, ada_s = ada[:n_b], ada[n_b:]
    bias = _band_bias(rel_bias[0])

    qa, ka, va, qb, kb, vb, lft = _in_proj(x_prompt, ada_p, w_main, wf_t, b_forget[0], bb=1, t=1024)
    f_p = _cumsum_lanes(lft)
    fox = _fox_prompt(qa, ka, va, f_p)
    band = _band_prompt(qb, kb, vb, bias)
    x1 = _out_proj(fox, band, x_prompt, ada_p, w_out_b, ln_mix_g[0], ln_mix_b[0], bb=1, t=512)
    y_p = _mlp(x1, ada_p, w_up_b, b_up[0], w_down_b, ln_mlp_g[0], ln_mlp_b[0], bb=1, t=1024)

    heads = lambda a: a.reshape(1, a.shape[0], a.shape[1], N_HEADS, D_HEAD)
    band_w = min(BAND_ROWS, seq)
    p_states = (heads(ka), heads(va), jnp.swapaxes(lft, 1, 2)[None],
                heads(kb[:, seq - band_w:]), heads(vb[:, seq - band_w:]))

    qa_s, ka_s, va_s, qb_s, kb_s, vb_s, lft_s = _in_proj(
        x_sample, ada_s, w_main, wf_t, b_forget[0], bb=n_bs, t=n_new)
    logf_s = jnp.transpose(lft_s.reshape(N_HEADS, n_bs, n_new), (1, 0, 2))
    total = past + n_new
    padded = -(-total // 128) * 128
    logf_all = jnp.concatenate(
        [jnp.swapaxes(cache_fox_logf[0], 1, 2), logf_s, jnp.zeros((n_bs, N_HEADS, padded - total), F32)], axis=2)
    f_s = _cumsum_lanes(logf_all)
    f_q = jnp.swapaxes(f_s[:, :, past:total], 1, 2)
    fox_s = _fox_sample(qa_s, cache_fox_k[0].reshape(n_bs, past, gw), cache_fox_v[0].reshape(n_bs, past, gw),
                        ka_s, va_s, f_s, f_q)
    band_past = cache_band_k.shape[2]
    band_s = _band_sample(qb_s, cache_band_k[0].reshape(n_bs, band_past, gw),
                          cache_band_v[0].reshape(n_bs, band_past, gw), kb_s, vb_s, bias)
    x1_s = _out_proj(fox_s, band_s, x_sample, ada_s, w_out_b, ln_mix_g[0], ln_mix_b[0], bb=n_bs, t=n_new)
    y_s = _mlp(x1_s, ada_s, w_up_b, b_up[0], w_down_b, ln_mlp_g[0], ln_mlp_b[0], bb=n_bs, t=n_new)
    s_states = (heads(ka_s), heads(va_s), jnp.swapaxes(logf_s, 1, 2)[None], heads(kb_s), heads(vb_s))

    return (y_p, y_s) + p_states + s_states
```

```python
import functools
import math

import jax
import jax.numpy as jnp
from jax import lax
from jax.experimental import pallas as pl
from jax.experimental.pallas import tpu as pltpu

D_HEAD = 128
N_HEADS = 8
GROUP_WIDTH = N_HEADS * D_HEAD
CHUNK = 64
BAND_ROWS = 512
BAND_Q_BLOCK = 256
LOOKAHEAD = 1
MAX_REL = 128
N_REL = 2 * MAX_REL + 1
ALPHA = 2.0 ** 0.25
LN_EPS = 1e-5
NEG = -1e30
LOG2E = math.log2(math.e)
Q_SCALE = D_HEAD ** -0.5 * LOG2E

VMEM_LIMIT_BYTES = 60 * 1024 * 1024

F32 = jnp.float32
BF16 = jnp.bfloat16


def _params(n_axes, vmem=None):
    return pltpu.CompilerParams(dimension_semantics=("arbitrary",) * n_axes, vmem_limit_bytes=vmem)


def _norm(x):
    mu = jnp.mean(x, axis=-1, keepdims=True)
    xc = x - mu
    var = jnp.mean(xc * xc, axis=-1, keepdims=True)
    return xc * lax.rsqrt(var + LN_EPS)


def _dot_nt(a, b):
    return lax.dot_general(a, b, (((1,), (1,)), ((), ())), preferred_element_type=F32)


def _dot(a, b):
    return jnp.dot(a, b, preferred_element_type=F32)


ROW_CHUNK = 256


def _for_row_chunks(bb, t, fn):
    rows = min(t, ROW_CHUNK)
    for b in range(bb):
        if t == rows:
            fn(b, 0, rows)
        else:
            def body(c, carry, b=b):
                fn(b, pl.multiple_of(c * rows, rows), rows)
                return carry
            lax.fori_loop(0, t // rows, body, 0)


def _ada_kernel(c_ref, w_ref, b_ref, o_ref):
    a = jax.nn.silu(c_ref[...]).astype(BF16)
    o_ref[...] = _dot(a, w_ref[...].astype(BF16)) + b_ref[...]


def _ada(c, w_ada, b_ada, tn=1024):
    n_rows, d = c.shape
    n_out = w_ada.shape[1]
    return pl.pallas_call(
        _ada_kernel,
        grid=(n_out // tn,),
        in_specs=[pl.BlockSpec((n_rows, d), lambda j: (0, 0)),
                  pl.BlockSpec((d, tn), lambda j: (0, j)),
                  pl.BlockSpec((1, tn), lambda j: (0, j))],
        out_specs=pl.BlockSpec((n_rows, tn), lambda j: (0, j)),
        out_shape=jax.ShapeDtypeStruct((n_rows, n_out), F32),
        compiler_params=_params(1, 40 * 1024 * 1024),
        name="ada",
    )(c, w_ada, b_ada.reshape(1, n_out))


def _band_bias_kernel(tab_ref, o_ref):
    h = pl.program_id(0)
    t = BAND_ROWS
    w = 2 * t
    kidx = lax.broadcasted_iota(jnp.int32, (1, w), 1)
    d = jnp.where(kidx < t, kidx, kidx - w)
    idx_cur = jnp.clip(d, -MAX_REL, MAX_REL) + MAX_REL
    idx_prev = jnp.clip(d - t, -MAX_REL, MAX_REL) + MAX_REL

    def fill(r, rows):
        rc, rp = rows
        val = tab_ref[h, r] * LOG2E
        return jnp.where(idx_cur == r, val, rc), jnp.where(idx_prev == r, val, rp)

    rc, rp = lax.fori_loop(0, N_REL, fill, (jnp.zeros((1, w), F32), jnp.zeros((1, w), F32)))
    n_q = o_ref.shape[2]
    qrow = lax.broadcasted_iota(jnp.int32, (n_q, w), 0)

    def toeplitz(row):
        x = jnp.broadcast_to(row, (n_q, w))
        for bit in range(n_q.bit_length() - 1):
            x = jnp.where(((qrow >> bit) & 1) == 1, pltpu.roll(x, 1 << bit, 1), x)
        return x[:, :t]

    qc = lax.broadcasted_iota(jnp.int32, (n_q, t), 0) // CHUNK
    kc = lax.broadcasted_iota(jnp.int32, (n_q, t), 1) // CHUNK
    o_ref[0, 1] = jnp.where(kc <= qc, toeplitz(rc), NEG)
    o_ref[0, 0] = jnp.where(kc >= qc, toeplitz(rp), NEG)


def _band_bias(table, n_q):
    t = BAND_ROWS
    assert n_q & (n_q - 1) == 0 and n_q <= t
    return pl.pallas_call(
        _band_bias_kernel,
        grid=(N_HEADS,),
        in_specs=[pl.BlockSpec(memory_space=pltpu.SMEM)],
        out_specs=pl.BlockSpec((1, 2, n_q, t), lambda h: (h, 0, 0, 0)),
        out_shape=jax.ShapeDtypeStruct((N_HEADS, 2, n_q, t), F32),
        compiler_params=_params(1, 40 * 1024 * 1024),
        name="band_bias",
    )(table)


def _in_proj_kernel(x_ref, ada_ref, w_ref, wf_ref, bf_ref,
                    qa_ref, ka_ref, va_ref, qb_ref, kb_ref, vb_ref, lf_ref, u_ref, *, tiles_per_out):
    j = pl.program_id(2)
    bb, t, d = x_ref.shape

    @pl.when(j == 0)
    def _():
        def modulate(b, start, rows):
            u = _norm(x_ref[b, pl.ds(start, rows), :]) * (1.0 + ada_ref[b, 1:2, :]) + ada_ref[b, 0:1, :]
            u_ref[pl.ds(b * t + start, rows), :] = u.astype(BF16)

        _for_row_chunks(bb, t, modulate)
        f = _dot_nt(wf_ref[...], u_ref[...])
        lf_ref[0] = jax.nn.log_sigmoid(f[:N_HEADS] + bf_ref[...])

    tn = w_ref.shape[1]
    outs = ((qa_ref, Q_SCALE), (ka_ref, None), (va_ref, None),
            (qb_ref, Q_SCALE), (kb_ref, None), (vb_ref, None))
    for o, (ref, scale) in enumerate(outs):
        @pl.when(j // tiles_per_out == o)
        def _(ref=ref, scale=scale):
            acc = _dot(u_ref[...], w_ref[...])
            val = acc if scale is None else acc * scale
            ref[...] = val.astype(ref.dtype).reshape(bb, t, tn)


def _in_proj(x, ada, w_main, wf_t, b_forget, *, bb, t, tn=512):
    n_b, s, d = x.shape
    m = bb * t
    tiles_per_out = GROUP_WIDTH // tn
    n_j = 6 * tiles_per_out

    n_i = s // t
    n_blocks = (n_b // bb) * n_i

    def out_spec(o):
        def index(b, i, j):
            rel = j - o * tiles_per_out
            lin = b * n_i + i
            move = jnp.logical_and(rel >= tiles_per_out, lin < n_blocks - 1)
            lin = lin + move.astype(jnp.int32)
            col = jnp.where(move, 0, jnp.clip(rel, 0, tiles_per_out - 1))
            return lin // n_i, lin % n_i, col
        return pl.BlockSpec((bb, t, tn), index)

    tok = lambda dt: jax.ShapeDtypeStruct((n_b, s, GROUP_WIDTH), dt)
    return pl.pallas_call(
        functools.partial(_in_proj_kernel, tiles_per_out=tiles_per_out),
        grid=(n_b // bb, s // t, n_j),
        in_specs=[pl.BlockSpec((bb, t, d), lambda b, i, j: (b, i, 0)),
                  pl.BlockSpec((bb, 6, d), lambda b, i, j: (b, 0, 0)),
                  pl.BlockSpec((d, tn), lambda b, i, j: (0, j)),
                  pl.BlockSpec(wf_t.shape, lambda b, i, j: (0, 0)),
                  pl.BlockSpec((N_HEADS, 1), lambda b, i, j: (0, 0))],
        out_specs=[out_spec(o) for o in range(6)]
        + [pl.BlockSpec((1, N_HEADS, m), lambda b, i, j: (b, 0, i))],
        out_shape=[tok(BF16), tok(F32), tok(F32), tok(BF16), tok(F32), tok(F32),
                   jax.ShapeDtypeStruct((n_b // bb, N_HEADS, (s // t) * m), F32)],
        scratch_shapes=[pltpu.VMEM((m, d), BF16)],
        compiler_params=_params(3, VMEM_LIMIT_BYTES),
        name="in_proj",
    )(x, ada, w_main, wf_t, b_forget.reshape(N_HEADS, 1))


def _cumsum_kernel(x_ref, o_ref):
    x = x_ref[0]
    n = x.shape[1]
    lane = lax.broadcasted_iota(jnp.int32, x.shape, 1)
    shift = 1
    while shift < n:
        x = x + jnp.where(lane >= shift, pltpu.roll(x, shift, 1), 0.0)
        shift *= 2
    o_ref[0] = x * LOG2E


def _cumsum_lanes(x):
    n_b, h, n = x.shape
    spec = pl.BlockSpec((1, h, n), lambda b: (b, 0, 0))
    return pl.pallas_call(
        _cumsum_kernel, grid=(n_b,), in_specs=[spec], out_specs=spec,
        out_shape=jax.ShapeDtypeStruct(x.shape, F32),
        compiler_params=_params(1), name="cumsum",
    )(x)


def _pick_head(fcol, h):
    lane = lax.broadcasted_iota(jnp.int32, fcol.shape, 1)
    return jnp.sum(jnp.where(lane == h, fcol, 0.0), axis=1, keepdims=True)


def _forget_columns(f, *, query):
    hi = f.astype(BF16).astype(F32)
    mid = (f - hi).astype(BF16).astype(F32)
    lo = (f - hi - mid).astype(BF16).astype(F32)
    pieces = (hi, mid, lo) if query else (-hi, -mid, -lo)
    first = 0 if query else 3
    lane = lax.broadcasted_iota(jnp.int32, (f.shape[0], D_HEAD), 1)
    ext = jnp.where((lane >= 3 - first) & (lane < 6 - first), 1.0, 0.0)
    for n, piece in enumerate(pieces):
        ext = jnp.where(lane == first + n, piece, ext)
    return ext.astype(BF16)


def _fox_prompt_kernel(q_ref, k_ref, v_ref, fkey_ref, fcol_ref, o_ref,
                       kx_ref, vb_ref, qx_ref, m_ref, l_ref, acc_ref, *, blk, strip, hp):
    hg = pl.program_id(1)
    qi = pl.program_id(2)
    lanes = D_HEAD
    n_col = blk // lanes
    n_blk = k_ref.shape[1] // blk
    head_lanes = [slice(hh * lanes, (hh + 1) * lanes) for hh in range(hp)]

    @pl.when(qi == 0)
    def _():
        for hh in range(hp):
            vb_ref[hh] = v_ref[0, :, head_lanes[hh]].astype(BF16)
            kx_ref[hh, :, :lanes] = k_ref[0, :, head_lanes[hh]].astype(BF16)
            for c in range(n_blk):
                rows = slice(c * blk, (c + 1) * blk)
                f_keys = _pick_head(fkey_ref[0, rows, :], hg * hp + hh)
                kx_ref[hh, rows, lanes:] = _forget_columns(f_keys, query=False)

    for hh in range(hp):
        qx_ref[hh, :, :lanes] = q_ref[0, :, head_lanes[hh]]
        qx_ref[hh, :, lanes:] = _forget_columns(_pick_head(fcol_ref[0], hg * hp + hh), query=True)
    m_ref[...] = jnp.full(m_ref.shape, NEG, F32)
    l_ref[...] = jnp.zeros(l_ref.shape, F32)
    acc_ref[...] = jnp.zeros(acc_ref.shape, F32)

    def scores(hh, j):
        start = pl.multiple_of(j * blk, blk)
        return _dot_nt(qx_ref[hh], kx_ref[hh, pl.ds(start, blk), :])

    def softmax(hh, s_all, masked):
        p_strips = []
        for r in range(blk // strip):
            rows = slice(r * strip, (r + 1) * strip)
            cols = []
            for cc in range(n_col):
                k0 = cc * lanes
                if masked and k0 >= (r + 1) * strip:
                    continue
                s = s_all[rows, k0:k0 + lanes]
                if masked and k0 + lanes - 1 > r * strip:
                    row = lax.broadcasted_iota(jnp.int32, s.shape, 0) + r * strip
                    col = lax.broadcasted_iota(jnp.int32, s.shape, 1) + k0
                    s = jnp.where(col <= row, s, NEG)
                cols.append(s)
            m_old = m_ref[hh, rows]
            m_blk = functools.reduce(jnp.maximum, cols)
            m_new = jnp.maximum(m_old, jnp.broadcast_to(jnp.max(m_blk, axis=1, keepdims=True), m_old.shape))
            ps = [jnp.exp2(s - m_new) for s in cols]
            a = jnp.exp2(m_old - m_new)
            l_ref[hh, rows] = a * l_ref[hh, rows] + functools.reduce(jnp.add, ps)
            m_ref[hh, rows] = m_new
            acc_ref[hh, rows] = a * acc_ref[hh, rows]
            ps = [p.astype(BF16) for p in ps] + [jnp.zeros((strip, lanes), BF16)] * (n_col - len(ps))
            p_strips.append(jnp.concatenate(ps, axis=1))
        return jnp.concatenate(p_strips, axis=0)

    def values(hh, j, p):
        start = pl.multiple_of(j * blk, blk)
        acc_ref[hh] += _dot(p, vb_ref[hh, pl.ds(start, blk), :])

    def block_step(j, masked):
        s = [scores(hh, j) for hh in range(min(LOOKAHEAD, hp))]
        for hh in range(hp):
            if hh + LOOKAHEAD < hp:
                s.append(scores(hh + LOOKAHEAD, j))
            values(hh, j, softmax(hh, s[hh], masked))

    def loop_body(j, carry):
        block_step(j, False)
        return carry

    lax.fori_loop(0, qi, loop_body, 0)
    block_step(qi, True)
    for hh in range(hp):
        l = jnp.sum(l_ref[hh], axis=1, keepdims=True)
        o_ref[0, :, head_lanes[hh]] = (acc_ref[hh] / l).astype(o_ref.dtype)


def _fox_prompt(q, k, v, f, *, blk=512, strip=32, hp=4):
    n_b, s, _ = q.shape
    n_blk = s // blk
    n_hg = N_HEADS // hp
    fcol = jnp.swapaxes(f, 1, 2)
    kv_spec = pl.BlockSpec((1, s, hp * D_HEAD), lambda b, h, i: (b, 0, h))
    q_spec = pl.BlockSpec((1, blk, hp * D_HEAD), lambda b, h, i: (b, i, h))
    state = pltpu.VMEM((hp, blk, D_HEAD), F32)
    return pl.pallas_call(
        functools.partial(_fox_prompt_kernel, blk=blk, strip=strip, hp=hp),
        grid=(n_b, n_hg, n_blk),
        in_specs=[q_spec, kv_spec, kv_spec,
                  pl.BlockSpec((1, s, N_HEADS), lambda b, h, i: (b, 0, 0)),
                  pl.BlockSpec((1, blk, N_HEADS), lambda b, h, i: (b, i, 0))],
        out_specs=q_spec,
        out_shape=jax.ShapeDtypeStruct((n_b, s, GROUP_WIDTH), BF16),
        scratch_shapes=[pltpu.VMEM((hp, s, 2 * D_HEAD), BF16), pltpu.VMEM((hp, s, D_HEAD), BF16),
                        pltpu.VMEM((hp, blk, 2 * D_HEAD), BF16), state, state, state],
        compiler_params=_params(3, VMEM_LIMIT_BYTES),
        name="fox_prompt",
    )(q, k, v, fcol, fcol)


def _band_prompt_kernel(q_ref, k_ref, v_ref, bias_ref, o_ref, kb_ref, vb_ref, *, tb, strip, hp):
    i = pl.program_id(2)
    past = BAND_ROWS
    lanes = D_HEAD
    win = past + tb
    head_lanes = [slice(hh * lanes, (hh + 1) * lanes) for hh in range(hp)]

    @pl.when(i == 0)
    def _():
        for hh in range(hp):
            kb_ref[hh, :past, :] = jnp.zeros((past, D_HEAD), BF16)
            vb_ref[hh, :past, :] = jnp.zeros((past, D_HEAD), BF16)
            kb_ref[hh, past:, :] = k_ref[0, :, head_lanes[hh]].astype(BF16)
            vb_ref[hh, past:, :] = v_ref[0, :, head_lanes[hh]].astype(BF16)

    start = pl.multiple_of(i * tb, tb)
    first_valid = past - start

    def scores(hh):
        return _dot_nt(q_ref[0, :, head_lanes[hh]], kb_ref[hh, pl.ds(start, win), :])

    def softmax(hh, s_all):
        p_strips, l_strips = [], []
        for r in range(tb // strip):
            rows = slice(r * strip, (r + 1) * strip)
            q_chunk = (r * strip) // CHUNK
            cols = {}
            for cc in range(win // lanes):
                k0 = cc * lanes
                if k0 < past:
                    if (k0 + lanes - 1) // CHUNK < q_chunk:
                        continue
                    s = s_all[rows, k0:k0 + lanes] + bias_ref[hh, 0, rows, k0:k0 + lanes]
                    col = lax.broadcasted_iota(jnp.int32, s.shape, 1) + k0
                    s = jnp.where(col >= first_valid, s, NEG)
                else:
                    if (k0 - past) // CHUNK > q_chunk:
                        continue
                    s = s_all[rows, k0:k0 + lanes] + bias_ref[hh, 1, rows, k0 - past:k0 - past + lanes]
                cols[cc] = s
            m_blk = functools.reduce(jnp.maximum, cols.values())
            m = jnp.broadcast_to(jnp.max(m_blk, axis=1, keepdims=True), m_blk.shape)
            ps = {cc: jnp.exp2(s - m) for cc, s in cols.items()}
            l_strips.append(functools.reduce(jnp.add, ps.values()))
            zero = jnp.zeros((strip, lanes), BF16)
            p_strips.append(jnp.concatenate(
                [ps[cc].astype(BF16) if cc in ps else zero for cc in range(win // lanes)], axis=1))
        return jnp.concatenate(p_strips, axis=0), jnp.concatenate(l_strips, axis=0)

    def values(hh, p, l_part):
        acc = _dot(p, vb_ref[hh, pl.ds(start, win), :])
        l = jnp.sum(l_part, axis=1, keepdims=True)
        o_ref[0, :, head_lanes[hh]] = (acc / l).astype(o_ref.dtype)

    s = [scores(hh) for hh in range(min(LOOKAHEAD, hp))]
    for hh in range(hp):
        if hh + LOOKAHEAD < hp:
            s.append(scores(hh + LOOKAHEAD))
        values(hh, *softmax(hh, s[hh]))


def _band_prompt(q, k, v, bias, *, tb=BAND_Q_BLOCK, strip=32, hp=4):
    n_b, s, _ = q.shape
    kv_spec = pl.BlockSpec((1, s, hp * D_HEAD), lambda h, b, i: (b, 0, h))
    q_spec = pl.BlockSpec((1, tb, hp * D_HEAD), lambda h, b, i: (b, i, h))
    kv_scratch = pltpu.VMEM((hp, BAND_ROWS + s, D_HEAD), BF16)
    return pl.pallas_call(
        functools.partial(_band_prompt_kernel, tb=tb, strip=strip, hp=hp),
        grid=(N_HEADS // hp, n_b, s // tb),
        in_specs=[q_spec, kv_spec, kv_spec,
                  pl.BlockSpec((hp, 2, tb, BAND_ROWS), lambda h, b, i: (h, 0, 0, 0))],
        out_specs=q_spec,
        out_shape=jax.ShapeDtypeStruct((n_b, s, GROUP_WIDTH), BF16),
        scratch_shapes=[kv_scratch, kv_scratch],
        compiler_params=_params(3, VMEM_LIMIT_BYTES),
        name="band_prompt",
    )(q, k, v, bias)


def _fox_sample_kernel(q_ref, kc_ref, vc_ref, kn_ref, vn_ref, fc_ref, fn_ref, fq_ref, o_ref,
                       m_ref, l_ref, acc_ref, *, tc):
    c = pl.program_id(1)
    n_new = q_ref.shape[1]
    lanes = D_HEAD

    @pl.when(c == 0)
    def _():
        m_ref[...] = jnp.full(m_ref.shape, NEG, F32)
        l_ref[...] = jnp.zeros(l_ref.shape, F32)
        acc_ref[...] = jnp.zeros(acc_ref.shape, F32)

    def update(h, s, v):
        m_old = m_ref[h][:, :1]
        m_new = jnp.maximum(m_old, jnp.max(s, axis=1, keepdims=True))
        p = jnp.exp2(s - m_new)
        a = jnp.exp2(m_old - m_new)
        l_ref[h] = a * l_ref[h] + jnp.sum(p, axis=1, keepdims=True)
        m_ref[h] = jnp.broadcast_to(m_new, (n_new, lanes))
        acc_ref[h] = a * acc_ref[h] + _dot(p.astype(BF16), v)

    for h in range(N_HEADS):
        hl = slice(h * lanes, (h + 1) * lanes)
        rows = pl.ds(h, tc, stride=N_HEADS)
        s = (_dot_nt(q_ref[0, :, hl], kc_ref[0, rows, :].astype(BF16))
             + fq_ref[0, :, h:h + 1] - fc_ref[0, 0, h:h + 1, :])
        update(h, s, vc_ref[0, rows, :].astype(BF16))

    @pl.when(c == pl.num_programs(1) - 1)
    def _():
        for h in range(N_HEADS):
            hl = slice(h * lanes, (h + 1) * lanes)
            s = (_dot_nt(q_ref[0, :, hl], kn_ref[0, :, hl].astype(BF16))
                 + fq_ref[0, :, h:h + 1] - fn_ref[0, h:h + 1, :])
            row = lax.broadcasted_iota(jnp.int32, s.shape, 0)
            col = lax.broadcasted_iota(jnp.int32, s.shape, 1)
            update(h, jnp.where(col <= row, s, NEG), vn_ref[0, :, hl].astype(BF16))
            o_ref[0, :, hl] = (acc_ref[h] / l_ref[h]).astype(o_ref.dtype)


def _fox_sample(q, k_cache, v_cache, k_new, v_new, f_cache, f_new, f_q, *, tc=1024):
    n_b, n_new, _ = q.shape
    past = k_cache.shape[1] // N_HEADS
    n_chunks = past // tc
    f_cache = jnp.swapaxes(f_cache.reshape(n_b, N_HEADS, n_chunks, tc), 1, 2)
    new_spec = pl.BlockSpec((1, n_new, GROUP_WIDTH), lambda b, c: (b, 0, 0))
    cache_spec = pl.BlockSpec((1, tc * N_HEADS, D_HEAD), lambda b, c: (b, c, 0))
    state = pltpu.VMEM((N_HEADS, n_new, D_HEAD), F32)
    return pl.pallas_call(
        functools.partial(_fox_sample_kernel, tc=tc),
        grid=(n_b, n_chunks),
        in_specs=[new_spec, cache_spec, cache_spec, new_spec, new_spec,
                  pl.BlockSpec((1, 1, N_HEADS, tc), lambda b, c: (b, c, 0, 0)),
                  pl.BlockSpec((1, N_HEADS, n_new), lambda b, c: (b, 0, 0)),
                  pl.BlockSpec((1, n_new, N_HEADS), lambda b, c: (b, 0, 0))],
        out_specs=new_spec,
        out_shape=jax.ShapeDtypeStruct((n_b, n_new, GROUP_WIDTH), BF16),
        scratch_shapes=[state, state, state],
        compiler_params=_params(2, 40 * 1024 * 1024),
        name="fox_sample",
    )(q, k_cache, v_cache, k_new, v_new, f_cache, f_new, f_q)


def _band_sample_kernel(q_ref, kc_ref, vc_ref, kn_ref, vn_ref, bias_ref, o_ref):
    n_new = q_ref.shape[1]
    past = kc_ref.shape[1] // N_HEADS
    lanes = D_HEAD
    for h in range(N_HEADS):
        hl = slice(h * lanes, (h + 1) * lanes)
        rows = pl.ds(h, past, stride=N_HEADS)
        q = q_ref[0, :, hl]
        s_c = _dot_nt(q, kc_ref[0, rows, :].astype(BF16)) + bias_ref[h, 0]
        s_n = _dot_nt(q, kn_ref[0, :, hl].astype(BF16)) + bias_ref[h, 1][:, :n_new]
        m = jnp.maximum(jnp.max(s_c, axis=1, keepdims=True), jnp.max(s_n, axis=1, keepdims=True))
        p_c = jnp.exp2(s_c - m)
        p_n = jnp.exp2(s_n - m)
        l = jnp.sum(p_c, axis=1, keepdims=True) + jnp.sum(p_n, axis=1, keepdims=True)
        acc = (_dot(p_c.astype(BF16), vc_ref[0, rows, :].astype(BF16))
               + _dot(p_n.astype(BF16), vn_ref[0, :, hl].astype(BF16)))
        o_ref[0, :, hl] = (acc / l).astype(o_ref.dtype)


def _band_sample(q, k_cache, v_cache, k_new, v_new, bias):
    n_b, n_new, _ = q.shape
    new_spec = pl.BlockSpec((1, n_new, GROUP_WIDTH), lambda b: (b, 0, 0))
    cache_spec = pl.BlockSpec((1,) + k_cache.shape[1:], lambda b: (b, 0, 0))
    return pl.pallas_call(
        _band_sample_kernel,
        grid=(n_b,),
        in_specs=[new_spec, cache_spec, cache_spec, new_spec, new_spec,
                  pl.BlockSpec((N_HEADS, 2, n_new, BAND_ROWS), lambda b: (0, 0, 0, 0))],
        out_specs=new_spec,
        out_shape=jax.ShapeDtypeStruct((n_b, n_new, GROUP_WIDTH), BF16),
        compiler_params=_params(1, 40 * 1024 * 1024),
        name="band_sample",
    )(q, k_cache, v_cache, k_new, v_new, bias)


def _out_proj_kernel(fox_ref, band_ref, x_ref, ada_ref, w_ref, g_ref, b_ref, o_ref):
    bb, t, d = x_ref.shape
    gw = fox_ref.shape[2]
    fox = fox_ref[...].reshape(bb * t, gw)
    band = band_ref[...].reshape(bb * t, gw)
    y = _dot(fox, w_ref[:gw, :]) + _dot(band, w_ref[gw:, :])
    o_ref[...] = y.reshape(bb, t, d)

    def post_ln(b, start, rows):
        sl = pl.ds(start, rows)
        z = ALPHA * x_ref[b, sl, :] + ada_ref[b, 2:3, :] * o_ref[b, sl, :]
        o_ref[b, sl, :] = _norm(z) * g_ref[...] + b_ref[...]

    _for_row_chunks(bb, t, post_ln)


def _out_proj(fox, band, x, ada, w_out, g, b, *, bb, t):
    n_b, s, d = x.shape
    tok = lambda w: pl.BlockSpec((bb, t, w), lambda bi, i: (bi, i, 0))
    vec = pl.BlockSpec((1, d), lambda bi, i: (0, 0))
    return pl.pallas_call(
        _out_proj_kernel,
        grid=(n_b // bb, s // t),
        in_specs=[tok(GROUP_WIDTH), tok(GROUP_WIDTH), tok(d),
                  pl.BlockSpec((bb, 6, d), lambda bi, i: (bi, 0, 0)),
                  pl.BlockSpec(w_out.shape, lambda bi, i: (0, 0)), vec, vec],
        out_specs=tok(d),
        out_shape=jax.ShapeDtypeStruct(x.shape, F32),
        compiler_params=_params(2, VMEM_LIMIT_BYTES),
        name="out_proj",
    )(fox, band, x, ada, w_out, g.reshape(1, d), b.reshape(1, d))


def _mlp_kernel(x_ref, ada_ref, wu_ref, bu_ref, wd_ref, g_ref, b_ref, o_ref, u_ref):
    f = pl.program_id(2)
    bb, t, d = x_ref.shape

    @pl.when(f == 0)
    def _():
        def modulate(b, start, rows):
            u = _norm(x_ref[b, pl.ds(start, rows), :]) * (1.0 + ada_ref[b, 4:5, :]) + ada_ref[b, 3:4, :]
            u_ref[pl.ds(b * t + start, rows), :] = u.astype(BF16)

        _for_row_chunks(bb, t, modulate)
        o_ref[...] = jnp.zeros_like(o_ref)

    hid = jnp.maximum(_dot(u_ref[...], wu_ref[...]) + bu_ref[...], 0.0)
    hid = (hid * hid).astype(BF16)
    o_ref[...] += _dot(hid, wd_ref[...]).reshape(bb, t, d)

    @pl.when(f == pl.num_programs(2) - 1)
    def _():
        def post_ln(b, start, rows):
            sl = pl.ds(start, rows)
            z = ALPHA * x_ref[b, sl, :] + ada_ref[b, 5:6, :] * o_ref[b, sl, :]
            o_ref[b, sl, :] = _norm(z) * g_ref[...] + b_ref[...]

        _for_row_chunks(bb, t, post_ln)


def _mlp(x, ada, w_up, b_up, w_down, g, b, *, bb, t, tf=512):
    n_b, s, d = x.shape
    d_ff = w_up.shape[1]
    tok = pl.BlockSpec((bb, t, d), lambda bi, i, f: (bi, i, 0))
    vec = pl.BlockSpec((1, d), lambda bi, i, f: (0, 0))
    return pl.pallas_call(
        _mlp_kernel,
        grid=(n_b // bb, s // t, d_ff // tf),
        in_specs=[tok, pl.BlockSpec((bb, 6, d), lambda bi, i, f: (bi, 0, 0)),
                  pl.BlockSpec((d, tf), lambda bi, i, f: (0, f)),
                  pl.BlockSpec((1, tf), lambda bi, i, f: (0, f)),
                  pl.BlockSpec((tf, d), lambda bi, i, f: (f, 0)), vec, vec],
        out_specs=tok,
        out_shape=jax.ShapeDtypeStruct(x.shape, F32),
        scratch_shapes=[pltpu.VMEM((bb * t, d), BF16)],
        compiler_params=_params(3, VMEM_LIMIT_BYTES),
        name="mlp",
    )(x, ada, w_up, b_up.reshape(1, d_ff), w_down, g.reshape(1, d), b.reshape(1, d))


def kernel(x_prompt, x_sample, cache_fox_k, cache_fox_v, cache_fox_logf, cache_band_k, cache_band_v,
           c_prompt, c_sample, w_ada, b_ada, w_in, b_forget, rel_bias, w_out, ln_mix_g, ln_mix_b,
           w_up, b_up, w_down, ln_mlp_g, ln_mlp_b):
    assert w_in.shape[0] == 1, "single-layer step"
    n_b, seq, d = x_prompt.shape
    n_bs, n_new, _ = x_sample.shape
    past = cache_fox_k.shape[2]
    gw = GROUP_WIDTH

    w = w_in[0]
    w_main = jnp.concatenate([w[:, :3 * gw], w[:, 3 * gw + N_HEADS:]], axis=1).astype(BF16)
    wf_t = jnp.pad(w[:, 3 * gw:3 * gw + N_HEADS].T, ((0, 16 - N_HEADS), (0, 0))).astype(BF16)
    w_out_b = w_out[0].astype(BF16)
    w_up_b = w_up[0].astype(BF16)
    w_down_b = w_down[0].astype(BF16)

    ada = _ada(jnp.concatenate([c_prompt, c_sample], axis=0), w_ada[0], b_ada[0])
    ada = ada.reshape(n_b + n_bs, 6, d)
    ada_p, ada_s = ada[:n_b], ada[n_b:]
    bias = _band_bias(rel_bias[0], BAND_Q_BLOCK)

    qa, ka, va, qb, kb, vb, lft = _in_proj(x_prompt, ada_p, w_main, wf_t, b_forget[0], bb=1, t=1024)
    f_p = _cumsum_lanes(lft)
    fox = _fox_prompt(qa, ka, va, f_p)
    band = _band_prompt(qb, kb, vb, bias)
    x1 = _out_proj(fox, band, x_prompt, ada_p, w_out_b, ln_mix_g[0], ln_mix_b[0], bb=1, t=512)
    y_p = _mlp(x1, ada_p, w_up_b, b_up[0], w_down_b, ln_mlp_g[0], ln_mlp_b[0], bb=1, t=1024)

    heads = lambda a: a.reshape(1, a.shape[0], a.shape[1], N_HEADS, D_HEAD)
    band_w = min(BAND_ROWS, seq)
    p_states = (heads(ka), heads(va), jnp.swapaxes(lft, 1, 2)[None],
                heads(kb[:, seq - band_w:]), heads(vb[:, seq - band_w:]))

    qa_s, ka_s, va_s, qb_s, kb_s, vb_s, lft_s = _in_proj(
        x_sample, ada_s, w_main, wf_t, b_forget[0], bb=n_bs, t=n_new)
    logf_s = jnp.transpose(lft_s.reshape(N_HEADS, n_bs, n_new), (1, 0, 2))
    total = past + n_new
    padded = -(-total // 128) * 128
    logf_all = jnp.concatenate(
        [jnp.swapaxes(cache_fox_logf[0], 1, 2), logf_s, jnp.zeros((n_bs, N_HEADS, padded - total), F32)], axis=2)
    f_s = _cumsum_lanes(logf_all)
    f_new = f_s[:, :, past:total]
    rows_view = lambda cache: cache.reshape(n_bs, cache.shape[2] * N_HEADS, D_HEAD)
    fox_s = _fox_sample(qa_s, rows_view(cache_fox_k), rows_view(cache_fox_v), ka_s, va_s,
                        f_s[:, :, :past], f_new, jnp.swapaxes(f_new, 1, 2))
    band_s = _band_sample(qb_s, rows_view(cache_band_k), rows_view(cache_band_v), kb_s, vb_s, bias)
    x1_s = _out_proj(fox_s, band_s, x_sample, ada_s, w_out_b, ln_mix_g[0], ln_mix_b[0], bb=n_bs, t=n_new)
    y_s = _mlp(x1_s, ada_s, w_up_b, b_up[0], w_down_b, ln_mlp_g[0], ln_mlp_b[0], bb=n_bs, t=n_new)
    s_states = (heads(ka_s), heads(va_s), jnp.swapaxes(logf_s, 1, 2)[None], heads(kb_s), heads(vb_s))

    return (y_p, y_s) + p_states + s_states
```

```python
import functools
import math

import jax
import jax.numpy as jnp
from jax import lax
from jax.experimental import pallas as pl
from jax.experimental.pallas import tpu as pltpu

D_HEAD = 128
N_HEADS = 8
GROUP_WIDTH = N_HEADS * D_HEAD
CHUNK = 64
BAND_ROWS = 512
BAND_Q_BLOCK = 256
MAX_REL = 128
N_REL = 2 * MAX_REL + 1
ALPHA = 2.0 ** 0.25
LN_EPS = 1e-5
NEG = -1e30
LOG2E = math.log2(math.e)
Q_SCALE = D_HEAD ** -0.5 * LOG2E

VMEM_LIMIT_BYTES = 60 * 1024 * 1024

F32 = jnp.float32
BF16 = jnp.bfloat16


def _params(n_axes, vmem=None):
    return pltpu.CompilerParams(dimension_semantics=("arbitrary",) * n_axes, vmem_limit_bytes=vmem)


def _norm(x):
    mu = jnp.mean(x, axis=-1, keepdims=True)
    xc = x - mu
    var = jnp.mean(xc * xc, axis=-1, keepdims=True)
    return xc * lax.rsqrt(var + LN_EPS)


def _dot_nt(a, b):
    return lax.dot_general(a, b, (((1,), (1,)), ((), ())), preferred_element_type=F32)


def _dot(a, b):
    return jnp.dot(a, b, preferred_element_type=F32)


ROW_CHUNK = 256


def _row_chunks(bb, t):
    if t >= ROW_CHUNK:
        return [(slice(b, b + 1), slice(r, r + ROW_CHUNK), slice(b * t + r, b * t + r + ROW_CHUNK))
                for b in range(bb) for r in range(0, t, ROW_CHUNK)]
    nb = min(ROW_CHUNK // t, bb)
    return [(slice(b, b + nb), slice(0, t), slice(b * t, (b + nb) * t)) for b in range(0, bb, nb)]


def _ada_kernel(c_ref, w_ref, b_ref, o_ref):
    a = jax.nn.silu(c_ref[...]).astype(BF16)
    o_ref[...] = _dot(a, w_ref[...].astype(BF16)) + b_ref[...]


def _ada(c, w_ada, b_ada, tn=1024):
    n_rows, d = c.shape
    n_out = w_ada.shape[1]
    return pl.pallas_call(
        _ada_kernel,
        grid=(n_out // tn,),
        in_specs=[pl.BlockSpec((n_rows, d), lambda j: (0, 0)),
                  pl.BlockSpec((d, tn), lambda j: (0, j)),
                  pl.BlockSpec((1, tn), lambda j: (0, j))],
        out_specs=pl.BlockSpec((n_rows, tn), lambda j: (0, j)),
        out_shape=jax.ShapeDtypeStruct((n_rows, n_out), F32),
        compiler_params=_params(1, 40 * 1024 * 1024),
        name="ada",
    )(c, w_ada, b_ada.reshape(1, n_out))


def _band_bias_kernel(tab_ref, o_ref):
    h = pl.program_id(0)
    t = BAND_ROWS
    w = 2 * t
    kidx = lax.broadcasted_iota(jnp.int32, (1, w), 1)
    d = jnp.where(kidx < t, kidx, kidx - w)
    idx_cur = jnp.clip(d, -MAX_REL, MAX_REL) + MAX_REL
    idx_prev = jnp.clip(d - t, -MAX_REL, MAX_REL) + MAX_REL

    def fill(r, rows):
        rc, rp = rows
        val = tab_ref[h, r] * LOG2E
        return jnp.where(idx_cur == r, val, rc), jnp.where(idx_prev == r, val, rp)

    rc, rp = lax.fori_loop(0, N_REL, fill, (jnp.zeros((1, w), F32), jnp.zeros((1, w), F32)))
    n_q = o_ref.shape[2]
    qrow = lax.broadcasted_iota(jnp.int32, (n_q, w), 0)

    def toeplitz(row):
        x = jnp.broadcast_to(row, (n_q, w))
        for bit in range(n_q.bit_length() - 1):
            x = jnp.where(((qrow >> bit) & 1) == 1, pltpu.roll(x, 1 << bit, 1), x)
        return x[:, :t]

    qc = lax.broadcasted_iota(jnp.int32, (n_q, t), 0) // CHUNK
    kc = lax.broadcasted_iota(jnp.int32, (n_q, t), 1) // CHUNK
    o_ref[0, 1] = jnp.where(kc <= qc, toeplitz(rc), NEG)
    o_ref[0, 0] = jnp.where(kc >= qc, toeplitz(rp), NEG)


def _band_bias(table, n_q):
    t = BAND_ROWS
    assert n_q & (n_q - 1) == 0 and n_q <= t
    return pl.pallas_call(
        _band_bias_kernel,
        grid=(N_HEADS,),
        in_specs=[pl.BlockSpec(memory_space=pltpu.SMEM)],
        out_specs=pl.BlockSpec((1, 2, n_q, t), lambda h: (h, 0, 0, 0)),
        out_shape=jax.ShapeDtypeStruct((N_HEADS, 2, n_q, t), F32),
        compiler_params=_params(1, 40 * 1024 * 1024),
        name="band_bias",
    )(table)


def _in_proj_kernel(x_ref, ada_ref, w_ref, wf_ref, bf_ref,
                    qa_ref, ka_ref, va_ref, qb_ref, kb_ref, vb_ref, kbt_ref, vbt_ref, lf_ref, u_ref,
                    *, tiles_per_out):
    j = pl.program_id(2)
    bb, t, d = x_ref.shape
    tn = w_ref.shape[1]
    tail = kbt_ref.shape[1]
    in_tail_block = pl.program_id(1) == pl.num_programs(1) - 1

    @pl.when(j == 0)
    def _():
        for bs, rs, flat in _row_chunks(bb, t):
            x = x_ref[bs, rs, :]
            u = _norm(x) * (1.0 + ada_ref[bs, 1:2, :]) + ada_ref[bs, 0:1, :]
            u = u.reshape(flat.stop - flat.start, d).astype(BF16)
            u_ref[flat, :] = u
            qa_ref[bs, rs, :] = (_dot(u, w_ref[...]) * Q_SCALE).astype(qa_ref.dtype).reshape(x.shape[:2] + (tn,))
        f = _dot_nt(wf_ref[...], u_ref[...])
        lf_ref[0] = jax.nn.log_sigmoid(f[:N_HEADS] + bf_ref[...])

    outs = ((qa_ref, Q_SCALE, None), (ka_ref, None, None), (va_ref, None, None),
            (qb_ref, Q_SCALE, None), (kb_ref, None, kbt_ref), (vb_ref, None, vbt_ref))
    for o, (ref, scale, tail_ref) in enumerate(outs):
        @pl.when(jnp.logical_and(j // tiles_per_out == o, j > 0))
        def _(ref=ref, scale=scale, tail_ref=tail_ref):
            acc = _dot(u_ref[...], w_ref[...])
            val = acc if scale is None else acc * scale
            ref[...] = val.astype(ref.dtype).reshape(bb, t, tn)
            if tail_ref is not None:
                @pl.when(in_tail_block)
                def _():
                    tail_ref[...] = acc.reshape(bb, t, tn)[:, t - tail:, :]


def _in_proj(x, ada, w_main, wf_t, b_forget, *, bb, t, tn=512):
    n_b, s, d = x.shape
    m = bb * t
    tiles_per_out = GROUP_WIDTH // tn
    n_j = 6 * tiles_per_out

    n_i = s // t
    n_blocks = (n_b // bb) * n_i

    def out_spec(o):
        def index(b, i, j):
            rel = j - o * tiles_per_out
            lin = b * n_i + i
            move = jnp.logical_and(rel >= tiles_per_out, lin < n_blocks - 1)
            lin = lin + move.astype(jnp.int32)
            col = jnp.where(move, 0, jnp.clip(rel, 0, tiles_per_out - 1))
            return lin // n_i, lin % n_i, col
        return pl.BlockSpec((bb, t, tn), index)

    tail = min(BAND_ROWS, s)

    def tail_spec(o):
        def index(b, i, j):
            col = jnp.where(i == n_i - 1, jnp.clip(j - o * tiles_per_out, 0, tiles_per_out - 1), 0)
            return b, 0, col
        return pl.BlockSpec((bb, tail, tn), index)

    tok = lambda dt: jax.ShapeDtypeStruct((n_b, s, GROUP_WIDTH), dt)
    tail_shape = jax.ShapeDtypeStruct((n_b, tail, GROUP_WIDTH), F32)
    return pl.pallas_call(
        functools.partial(_in_proj_kernel, tiles_per_out=tiles_per_out),
        grid=(n_b // bb, s // t, n_j),
        in_specs=[pl.BlockSpec((bb, t, d), lambda b, i, j: (b, i, 0)),
                  pl.BlockSpec((bb, 6, d), lambda b, i, j: (b, 0, 0)),
                  pl.BlockSpec((d, tn), lambda b, i, j: (0, j)),
                  pl.BlockSpec(wf_t.shape, lambda b, i, j: (0, 0)),
                  pl.BlockSpec((N_HEADS, 1), lambda b, i, j: (0, 0))],
        out_specs=[out_spec(o) for o in range(6)] + [tail_spec(4), tail_spec(5)]
        + [pl.BlockSpec((1, N_HEADS, m), lambda b, i, j: (b, 0, i))],
        out_shape=[tok(BF16), tok(F32), tok(F32), tok(BF16), tok(BF16), tok(BF16), tail_shape, tail_shape,
                   jax.ShapeDtypeStruct((n_b // bb, N_HEADS, (s // t) * m), F32)],
        scratch_shapes=[pltpu.VMEM((m, d), BF16)],
        compiler_params=_params(3, VMEM_LIMIT_BYTES),
        name="in_proj",
    )(x, ada, w_main, wf_t, b_forget.reshape(N_HEADS, 1))


def _cumsum_kernel(x_ref, o_ref):
    x = x_ref[0]
    n = x.shape[1]
    lane = lax.broadcasted_iota(jnp.int32, x.shape, 1)
    shift = 1
    while shift < n:
        x = x + jnp.where(lane >= shift, pltpu.roll(x, shift, 1), 0.0)
        shift *= 2
    o_ref[0] = x * LOG2E


def _cumsum_lanes(x):
    n_b, h, n = x.shape
    spec = pl.BlockSpec((1, h, n), lambda b: (b, 0, 0))
    return pl.pallas_call(
        _cumsum_kernel, grid=(n_b,), in_specs=[spec], out_specs=spec,
        out_shape=jax.ShapeDtypeStruct(x.shape, F32),
        compiler_params=_params(1), name="cumsum",
    )(x)


def _pick_head(fcol, h):
    lane = lax.broadcasted_iota(jnp.int32, fcol.shape, 1)
    return jnp.sum(jnp.where(lane == h, fcol, 0.0), axis=1, keepdims=True)


def _fox_prompt_kernel(q_ref, k_ref, v_ref, frow_ref, fcol_ref, o_ref,
                       kb_ref, vb_ref, fq_ref, m_ref, l_ref, acc_ref, *, blk, strip, hp):
    hg = pl.program_id(1)
    qi = pl.program_id(2)
    lanes = D_HEAD
    n_col = blk // lanes
    head_lanes = [slice(hh * lanes, (hh + 1) * lanes) for hh in range(hp)]

    @pl.when(qi == 0)
    def _():
        for hh in range(hp):
            kb_ref[hh] = k_ref[0, :, head_lanes[hh]].astype(BF16)
            vb_ref[hh] = v_ref[0, :, head_lanes[hh]].astype(BF16)

    for hh in range(hp):
        fq_ref[hh] = jnp.broadcast_to(_pick_head(fcol_ref[0], hg * hp + hh), (blk, lanes))
    m_ref[...] = jnp.full(m_ref.shape, NEG, F32)
    l_ref[...] = jnp.zeros(l_ref.shape, F32)
    acc_ref[...] = jnp.zeros(acc_ref.shape, F32)

    def scores(hh, j):
        start = pl.multiple_of(j * blk, blk)
        return _dot_nt(q_ref[0, :, head_lanes[hh]], kb_ref[hh, pl.ds(start, blk), :])

    def softmax(hh, j, s_all, masked):
        f_keys = frow_ref[hh, j]
        p_strips = []
        for r in range(blk // strip):
            rows = slice(r * strip, (r + 1) * strip)
            fq = fq_ref[hh, rows]
            cols = []
            for cc in range(n_col):
                k0 = cc * lanes
                if masked and k0 >= (r + 1) * strip:
                    continue
                s = s_all[rows, k0:k0 + lanes] - f_keys[:, k0:k0 + lanes]
                if masked and k0 + lanes - 1 > r * strip:
                    row = lax.broadcasted_iota(jnp.int32, s.shape, 0) + r * strip
                    col = lax.broadcasted_iota(jnp.int32, s.shape, 1) + k0
                    s = jnp.where(col <= row, s, NEG)
                cols.append(s)
            m_old = m_ref[hh, rows]
            m_blk = functools.reduce(jnp.maximum, cols)
            m_new = jnp.maximum(m_old, fq + jnp.broadcast_to(jnp.max(m_blk, axis=1, keepdims=True), m_old.shape))
            shift = m_new - fq
            ps = [jnp.exp2(s - shift) for s in cols]
            a = jnp.exp2(m_old - m_new)
            l_ref[hh, rows] = a * l_ref[hh, rows] + functools.reduce(jnp.add, ps)
            m_ref[hh, rows] = m_new
            acc_ref[hh, rows] = a * acc_ref[hh, rows]
            ps = [p.astype(BF16) for p in ps] + [jnp.zeros((strip, lanes), BF16)] * (n_col - len(ps))
            p_strips.append(jnp.concatenate(ps, axis=1))
        return jnp.concatenate(p_strips, axis=0)

    def values(hh, j, p):
        start = pl.multiple_of(j * blk, blk)
        acc_ref[hh] += _dot(p, vb_ref[hh, pl.ds(start, blk), :])

    def block_step(j, masked):
        s_next = scores(0, j)
        for hh in range(hp):
            s_cur = s_next
            if hh + 1 < hp:
                s_next = scores(hh + 1, j)
            values(hh, j, softmax(hh, j, s_cur, masked))

    def loop_body(j, carry):
        block_step(j, False)
        return carry

    lax.fori_loop(0, qi, loop_body, 0)
    block_step(qi, True)
    for hh in range(hp):
        l = jnp.sum(l_ref[hh], axis=1, keepdims=True)
        o_ref[0, :, head_lanes[hh]] = (acc_ref[hh] / l).astype(o_ref.dtype)


def _fox_prompt(q, k, v, f, *, blk=512, strip=32, hp=4):
    n_b, s, _ = q.shape
    n_blk = s // blk
    n_hg = N_HEADS // hp
    frow = f.reshape(n_b * N_HEADS, n_blk, 1, blk)
    fcol = jnp.swapaxes(f, 1, 2)
    kv_spec = pl.BlockSpec((1, s, hp * D_HEAD), lambda b, h, i: (b, 0, h))
    q_spec = pl.BlockSpec((1, blk, hp * D_HEAD), lambda b, h, i: (b, i, h))
    state = pltpu.VMEM((hp, blk, D_HEAD), F32)
    return pl.pallas_call(
        functools.partial(_fox_prompt_kernel, blk=blk, strip=strip, hp=hp),
        grid=(n_b, n_hg, n_blk),
        in_specs=[q_spec, kv_spec, kv_spec,
                  pl.BlockSpec((hp, n_blk, 1, blk), lambda b, h, i: (b * n_hg + h, 0, 0, 0)),
                  pl.BlockSpec((1, blk, N_HEADS), lambda b, h, i: (b, i, 0))],
        out_specs=q_spec,
        out_shape=jax.ShapeDtypeStruct((n_b, s, GROUP_WIDTH), BF16),
        scratch_shapes=[pltpu.VMEM((hp, s, D_HEAD), BF16), pltpu.VMEM((hp, s, D_HEAD), BF16),
                        state, state, state, state],
        compiler_params=_params(3, VMEM_LIMIT_BYTES),
        name="fox_prompt",
    )(q, k, v, frow, fcol)


def _band_prompt_kernel(q_ref, k_ref, v_ref, bias_ref, o_ref, kb_ref, vb_ref, *, tb, nq, strip, hp):
    i = pl.program_id(2)
    past = BAND_ROWS
    lanes = D_HEAD
    win = past + tb
    head_lanes = [slice(hh * lanes, (hh + 1) * lanes) for hh in range(hp)]

    @pl.when(i == 0)
    def _():
        for hh in range(hp):
            kb_ref[hh, :past, :] = jnp.zeros((past, D_HEAD), BF16)
            vb_ref[hh, :past, :] = jnp.zeros((past, D_HEAD), BF16)
            kb_ref[hh, past:, :] = k_ref[0, :, head_lanes[hh]]
            vb_ref[hh, past:, :] = v_ref[0, :, head_lanes[hh]]

    chains = [(hh, qq) for qq in range(nq) for hh in range(hp)]

    def block_start(qq):
        return pl.multiple_of((i * nq + qq) * tb, tb)

    def scores(chain):
        hh, qq = chain
        q = q_ref[0, qq * tb:(qq + 1) * tb, head_lanes[hh]]
        return _dot_nt(q, kb_ref[hh, pl.ds(block_start(qq), win), :])

    def softmax(chain, s_all):
        hh, qq = chain
        first_valid = past - block_start(qq)
        p_strips, l_strips = [], []
        for r in range(tb // strip):
            rows = slice(r * strip, (r + 1) * strip)
            q_chunk = (r * strip) // CHUNK
            cols = {}
            for cc in range(win // lanes):
                k0 = cc * lanes
                if k0 < past:
                    if (k0 + lanes - 1) // CHUNK < q_chunk:
                        continue
                    s = s_all[rows, k0:k0 + lanes] + bias_ref[hh, 0, rows, k0:k0 + lanes]
                    col = lax.broadcasted_iota(jnp.int32, s.shape, 1) + k0
                    s = jnp.where(col >= first_valid, s, NEG)
                else:
                    if (k0 - past) // CHUNK > q_chunk:
                        continue
                    s = s_all[rows, k0:k0 + lanes] + bias_ref[hh, 1, rows, k0 - past:k0 - past + lanes]
                cols[cc] = s
            m_blk = functools.reduce(jnp.maximum, cols.values())
            m = jnp.broadcast_to(jnp.max(m_blk, axis=1, keepdims=True), m_blk.shape)
            ps = {cc: jnp.exp2(s - m) for cc, s in cols.items()}
            l_strips.append(functools.reduce(jnp.add, ps.values()))
            zero = jnp.zeros((strip, lanes), BF16)
            p_strips.append(jnp.concatenate(
                [ps[cc].astype(BF16) if cc in ps else zero for cc in range(win // lanes)], axis=1))
        return jnp.concatenate(p_strips, axis=0), jnp.concatenate(l_strips, axis=0)

    def values(chain, p, l_part):
        hh, qq = chain
        acc = _dot(p, vb_ref[hh, pl.ds(block_start(qq), win), :])
        l = jnp.sum(l_part, axis=1, keepdims=True)
        o_ref[0, qq * tb:(qq + 1) * tb, head_lanes[hh]] = (acc / l).astype(o_ref.dtype)

    s_next = scores(chains[0])
    for n, chain in enumerate(chains):
        s_cur = s_next
        if n + 1 < len(chains):
            s_next = scores(chains[n + 1])
        values(chain, *softmax(chain, s_cur))


def _band_prompt(q, k, v, bias, *, tb=BAND_Q_BLOCK, nq=2, strip=32, hp=4):
    n_b, s, _ = q.shape
    kv_spec = pl.BlockSpec((1, s, hp * D_HEAD), lambda h, b, i: (b, 0, h))
    q_spec = pl.BlockSpec((1, nq * tb, hp * D_HEAD), lambda h, b, i: (b, i, h))
    kv_scratch = pltpu.VMEM((hp, BAND_ROWS + s, D_HEAD), BF16)
    return pl.pallas_call(
        functools.partial(_band_prompt_kernel, tb=tb, nq=nq, strip=strip, hp=hp),
        grid=(N_HEADS // hp, n_b, s // (nq * tb)),
        in_specs=[q_spec, kv_spec, kv_spec,
                  pl.BlockSpec((hp, 2, tb, BAND_ROWS), lambda h, b, i: (h, 0, 0, 0))],
        out_specs=q_spec,
        out_shape=jax.ShapeDtypeStruct((n_b, s, GROUP_WIDTH), BF16),
        scratch_shapes=[kv_scratch, kv_scratch],
        compiler_params=_params(3, VMEM_LIMIT_BYTES),
        name="band_prompt",
    )(q, k, v, bias)


def _fox_sample_kernel(q_ref, kc_ref, vc_ref, kn_ref, vn_ref, fc_ref, fn_ref, fq_ref, o_ref,
                       m_ref, l_ref, acc_ref, *, tc):
    c = pl.program_id(1)
    n_new = q_ref.shape[1]
    lanes = D_HEAD

    @pl.when(c == 0)
    def _():
        m_ref[...] = jnp.full(m_ref.shape, NEG, F32)
        l_ref[...] = jnp.zeros(l_ref.shape, F32)
        acc_ref[...] = jnp.zeros(acc_ref.shape, F32)

    def update(h, s, v):
        m_old = m_ref[h][:, :1]
        m_new = jnp.maximum(m_old, jnp.max(s, axis=1, keepdims=True))
        p = jnp.exp2(s - m_new)
        a = jnp.exp2(m_old - m_new)
        l_ref[h] = a * l_ref[h] + jnp.sum(p, axis=1, keepdims=True)
        m_ref[h] = jnp.broadcast_to(m_new, (n_new, lanes))
        acc_ref[h] = a * acc_ref[h] + _dot(p.astype(BF16), v)

    for h in range(N_HEADS):
        hl = slice(h * lanes, (h + 1) * lanes)
        rows = pl.ds(h, tc, stride=N_HEADS)
        s = (_dot_nt(q_ref[0, :, hl], kc_ref[0, rows, :].astype(BF16))
             + fq_ref[0, :, h:h + 1] - fc_ref[0, 0, h:h + 1, :])
        update(h, s, vc_ref[0, rows, :].astype(BF16))

    @pl.when(c == pl.num_programs(1) - 1)
    def _():
        for h in range(N_HEADS):
            hl = slice(h * lanes, (h + 1) * lanes)
            s = (_dot_nt(q_ref[0, :, hl], kn_ref[0, :, hl].astype(BF16))
                 + fq_ref[0, :, h:h + 1] - fn_ref[0, h:h + 1, :])
            row = lax.broadcasted_iota(jnp.int32, s.shape, 0)
            col = lax.broadcasted_iota(jnp.int32, s.shape, 1)
            update(h, jnp.where(col <= row, s, NEG), vn_ref[0, :, hl].astype(BF16))
            o_ref[0, :, hl] = (acc_ref[h] / l_ref[h]).astype(o_ref.dtype)


def _fox_sample(q, k_cache, v_cache, k_new, v_new, f_cache, f_new, f_q, *, tc=1024):
    n_b, n_new, _ = q.shape
    past = k_cache.shape[1] // N_HEADS
    n_chunks = past // tc
    f_cache = jnp.swapaxes(f_cache.reshape(n_b, N_HEADS, n_chunks, tc), 1, 2)
    new_spec = pl.BlockSpec((1, n_new, GROUP_WIDTH), lambda b, c: (b, 0, 0))
    cache_spec = pl.BlockSpec((1, tc * N_HEADS, D_HEAD), lambda b, c: (b, c, 0))
    state = pltpu.VMEM((N_HEADS, n_new, D_HEAD), F32)
    return pl.pallas_call(
        functools.partial(_fox_sample_kernel, tc=tc),
        grid=(n_b, n_chunks),
        in_specs=[new_spec, cache_spec, cache_spec, new_spec, new_spec,
                  pl.BlockSpec((1, 1, N_HEADS, tc), lambda b, c: (b, c, 0, 0)),
                  pl.BlockSpec((1, N_HEADS, n_new), lambda b, c: (b, 0, 0)),
                  pl.BlockSpec((1, n_new, N_HEADS), lambda b, c: (b, 0, 0))],
        out_specs=new_spec,
        out_shape=jax.ShapeDtypeStruct((n_b, n_new, GROUP_WIDTH), BF16),
        scratch_shapes=[state, state, state],
        compiler_params=_params(2, 40 * 1024 * 1024),
        name="fox_sample",
    )(q, k_cache, v_cache, k_new, v_new, f_cache, f_new, f_q)


def _band_sample_kernel(q_ref, kc_ref, vc_ref, kn_ref, vn_ref, bias_ref, o_ref):
    n_new = q_ref.shape[1]
    past = kc_ref.shape[1] // N_HEADS
    lanes = D_HEAD
    for h in range(N_HEADS):
        hl = slice(h * lanes, (h + 1) * lanes)
        rows = pl.ds(h, past, stride=N_HEADS)
        q = q_ref[0, :, hl]
        s_c = _dot_nt(q, kc_ref[0, rows, :].astype(BF16)) + bias_ref[h, 0]
        s_n = _dot_nt(q, kn_ref[0, :, hl].astype(BF16)) + bias_ref[h, 1][:, :n_new]
        m = jnp.maximum(jnp.max(s_c, axis=1, keepdims=True), jnp.max(s_n, axis=1, keepdims=True))
        p_c = jnp.exp2(s_c - m)
        p_n = jnp.exp2(s_n - m)
        l = jnp.sum(p_c, axis=1, keepdims=True) + jnp.sum(p_n, axis=1, keepdims=True)
        acc = (_dot(p_c.astype(BF16), vc_ref[0, rows, :].astype(BF16))
               + _dot(p_n.astype(BF16), vn_ref[0, :, hl].astype(BF16)))
        o_ref[0, :, hl] = (acc / l).astype(o_ref.dtype)


def _band_sample(q, k_cache, v_cache, k_new, v_new, bias):
    n_b, n_new, _ = q.shape
    new_spec = pl.BlockSpec((1, n_new, GROUP_WIDTH), lambda b: (b, 0, 0))
    cache_spec = pl.BlockSpec((1,) + k_cache.shape[1:], lambda b: (b, 0, 0))
    return pl.pallas_call(
        _band_sample_kernel,
        grid=(n_b,),
        in_specs=[new_spec, cache_spec, cache_spec, new_spec, new_spec,
                  pl.BlockSpec((N_HEADS, 2, n_new, BAND_ROWS), lambda b: (0, 0, 0, 0))],
        out_specs=new_spec,
        out_shape=jax.ShapeDtypeStruct((n_b, n_new, GROUP_WIDTH), BF16),
        compiler_params=_params(1, 40 * 1024 * 1024),
        name="band_sample",
    )(q, k_cache, v_cache, k_new, v_new, bias)


def _out_proj_kernel(fox_ref, band_ref, x_ref, ada_ref, w_ref, g_ref, b_ref, o_ref):
    bb, t, d = x_ref.shape
    gw = fox_ref.shape[2]
    for bs, rs, flat in _row_chunks(bb, t):
        x = x_ref[bs, rs, :]
        rows = flat.stop - flat.start
        y = (_dot(fox_ref[bs, rs, :].reshape(rows, gw), w_ref[:gw, :])
             + _dot(band_ref[bs, rs, :].reshape(rows, gw), w_ref[gw:, :]))
        z = ALPHA * x + ada_ref[bs, 2:3, :] * y.reshape(x.shape)
        o_ref[bs, rs, :] = _norm(z) * g_ref[...] + b_ref[...]


def _out_proj(fox, band, x, ada, w_out, g, b, *, bb, t):
    n_b, s, d = x.shape
    tok = lambda w: pl.BlockSpec((bb, t, w), lambda bi, i: (bi, i, 0))
    vec = pl.BlockSpec((1, d), lambda bi, i: (0, 0))
    return pl.pallas_call(
        _out_proj_kernel,
        grid=(n_b // bb, s // t),
        in_specs=[tok(GROUP_WIDTH), tok(GROUP_WIDTH), tok(d),
                  pl.BlockSpec((bb, 6, d), lambda bi, i: (bi, 0, 0)),
                  pl.BlockSpec(w_out.shape, lambda bi, i: (0, 0)), vec, vec],
        out_specs=tok(d),
        out_shape=jax.ShapeDtypeStruct(x.shape, F32),
        compiler_params=_params(2, VMEM_LIMIT_BYTES),
        name="out_proj",
    )(fox, band, x, ada, w_out, g.reshape(1, d), b.reshape(1, d))


def _mlp_kernel(x_ref, ada_ref, wu_ref, bu_ref, wd_ref, g_ref, b_ref, o_ref, u_ref):
    f = pl.program_id(2)
    last = pl.num_programs(2) - 1
    bb, t, d = x_ref.shape
    chunks = _row_chunks(bb, t)

    def hidden(u):
        hid = jnp.maximum(_dot(u, wu_ref[...]) + bu_ref[...], 0.0)
        return (hid * hid).astype(BF16)

    @pl.when(f == 0)
    def _():
        for bs, rs, flat in chunks:
            u = _norm(x_ref[bs, rs, :]) * (1.0 + ada_ref[bs, 4:5, :]) + ada_ref[bs, 3:4, :]
            u = u.reshape(flat.stop - flat.start, d).astype(BF16)
            u_ref[flat, :] = u
            o_ref[bs, rs, :] = _dot(hidden(u), wd_ref[...]).reshape(x_ref[bs, rs, :].shape)

    @pl.when(jnp.logical_and(f > 0, f < last))
    def _():
        o_ref[...] += _dot(hidden(u_ref[...]), wd_ref[...]).reshape(bb, t, d)

    @pl.when(f == last)
    def _():
        for bs, rs, flat in chunks:
            x = x_ref[bs, rs, :]
            y = o_ref[bs, rs, :] + _dot(hidden(u_ref[flat, :]), wd_ref[...]).reshape(x.shape)
            z = ALPHA * x + ada_ref[bs, 5:6, :] * y
            o_ref[bs, rs, :] = _norm(z) * g_ref[...] + b_ref[...]


def _mlp(x, ada, w_up, b_up, w_down, g, b, *, bb, t, tf=512):
    n_b, s, d = x.shape
    d_ff = w_up.shape[1]
    assert d_ff // tf >= 2, "first and last hidden tiles are separate steps"
    tok = pl.BlockSpec((bb, t, d), lambda bi, i, f: (bi, i, 0))
    vec = pl.BlockSpec((1, d), lambda bi, i, f: (0, 0))
    return pl.pallas_call(
        _mlp_kernel,
        grid=(n_b // bb, s // t, d_ff // tf),
        in_specs=[tok, pl.BlockSpec((bb, 6, d), lambda bi, i, f: (bi, 0, 0)),
                  pl.BlockSpec((d, tf), lambda bi, i, f: (0, f)),
                  pl.BlockSpec((1, tf), lambda bi, i, f: (0, f)),
                  pl.BlockSpec((tf, d), lambda bi, i, f: (f, 0)), vec, vec],
        out_specs=tok,
        out_shape=jax.ShapeDtypeStruct(x.shape, F32),
        scratch_shapes=[pltpu.VMEM((bb * t, d), BF16)],
        compiler_params=_params(3, VMEM_LIMIT_BYTES),
        name="mlp",
    )(x, ada, w_up, b_up.reshape(1, d_ff), w_down, g.reshape(1, d), b.reshape(1, d))


def kernel(x_prompt, x_sample, cache_fox_k, cache_fox_v, cache_fox_logf, cache_band_k, cache_band_v,
           c_prompt, c_sample, w_ada, b_ada, w_in, b_forget, rel_bias, w_out, ln_mix_g, ln_mix_b,
           w_up, b_up, w_down, ln_mlp_g, ln_mlp_b):
    assert w_in.shape[0] == 1, "single-layer step"
    n_b, seq, d = x_prompt.shape
    n_bs, n_new, _ = x_sample.shape
    past = cache_fox_k.shape[2]
    gw = GROUP_WIDTH

    w = w_in[0]
    w_main = jnp.concatenate([w[:, :3 * gw], w[:, 3 * gw + N_HEADS:]], axis=1).astype(BF16)
    wf_t = jnp.pad(w[:, 3 * gw:3 * gw + N_HEADS].T, ((0, 16 - N_HEADS), (0, 0))).astype(BF16)
    w_out_b = w_out[0].astype(BF16)
    w_up_b = w_up[0].astype(BF16)
    w_down_b = w_down[0].astype(BF16)

    ada = _ada(jnp.concatenate([c_prompt, c_sample], axis=0), w_ada[0], b_ada[0])
    ada = ada.reshape(n_b + n_bs, 6, d)
    ada_p, ada_s = ada[:n_b], ada[n_b:]
    bias = _band_bias(rel_bias[0], BAND_Q_BLOCK)

    qa, ka, va, qb, kb, vb, kb_tail, vb_tail, lft = _in_proj(
        x_prompt, ada_p, w_main, wf_t, b_forget[0], bb=1, t=1024)
    f_p = _cumsum_lanes(lft)
    fox = _fox_prompt(qa, ka, va, f_p)
    band = _band_prompt(qb, kb, vb, bias)
    x1 = _out_proj(fox, band, x_prompt, ada_p, w_out_b, ln_mix_g[0], ln_mix_b[0], bb=1, t=512)
    y_p = _mlp(x1, ada_p, w_up_b, b_up[0], w_down_b, ln_mlp_g[0], ln_mlp_b[0], bb=1, t=1024)

    heads = lambda a: a.reshape(1, a.shape[0], a.shape[1], N_HEADS, D_HEAD)
    p_states = (heads(ka), heads(va), jnp.swapaxes(lft, 1, 2)[None], heads(kb_tail), heads(vb_tail))

    qa_s, ka_s, va_s, qb_s, kb_s, vb_s, kb_tail_s, vb_tail_s, lft_s = _in_proj(
        x_sample, ada_s, w_main, wf_t, b_forget[0], bb=n_bs, t=n_new)
    logf_s = jnp.transpose(lft_s.reshape(N_HEADS, n_bs, n_new), (1, 0, 2))
    total = past + n_new
    padded = -(-total // 128) * 128
    logf_all = jnp.concatenate(
        [jnp.swapaxes(cache_fox_logf[0], 1, 2), logf_s, jnp.zeros((n_bs, N_HEADS, padded - total), F32)], axis=2)
    f_s = _cumsum_lanes(logf_all)
    f_new = f_s[:, :, past:total]
    rows_view = lambda cache: cache.reshape(n_bs, cache.shape[2] * N_HEADS, D_HEAD)
    fox_s = _fox_sample(qa_s, rows_view(cache_fox_k), rows_view(cache_fox_v), ka_s, va_s,
                        f_s[:, :, :past], f_new, jnp.swapaxes(f_new, 1, 2))
    band_s = _band_sample(qb_s, rows_view(cache_band_k), rows_view(cache_band_v), kb_s, vb_s, bias)
    x1_s = _out_proj(fox_s, band_s, x_sample, ada_s, w_out_b, ln_mix_g[0], ln_mix_b[0], bb=n_bs, t=n_new)
    y_s = _mlp(x1_s, ada_s, w_up_b, b_up[0], w_down_b, ln_mlp_g[0], ln_mlp_b[0], bb=n_bs, t=n_new)
    s_states = (heads(ka_s), heads(va_s), jnp.swapaxes(logf_s, 1, 2)[None],
                heads(kb_tail_s), heads(vb_tail_s))

    return (y_p, y_s) + p_states + s_states
```

```python
import functools
import math

import jax
import jax.numpy as jnp
from jax import lax
from jax.experimental import pallas as pl
from jax.experimental.pallas import tpu as pltpu

D_HEAD = 128
N_HEADS = 8
GROUP_WIDTH = N_HEADS * D_HEAD
CHUNK = 64
BAND_ROWS = 512
BAND_Q_BLOCK = 256
MAX_REL = 128
N_REL = 2 * MAX_REL + 1
ALPHA = 2.0 ** 0.25
LN_EPS = 1e-5
NEG = -1e30
LOG2E = math.log2(math.e)
Q_SCALE = D_HEAD ** -0.5 * LOG2E

VMEM_LIMIT_BYTES = 60 * 1024 * 1024

F32 = jnp.float32
BF16 = jnp.bfloat16


def _params(n_axes, vmem=None):
    return pltpu.CompilerParams(dimension_semantics=("arbitrary",) * n_axes, vmem_limit_bytes=vmem)


def _norm(x):
    mu = jnp.mean(x, axis=-1, keepdims=True)
    xc = x - mu
    var = jnp.mean(xc * xc, axis=-1, keepdims=True)
    return xc * lax.rsqrt(var + LN_EPS)


def _dot_nt(a, b):
    return lax.dot_general(a, b, (((1,), (1,)), ((), ())), preferred_element_type=F32)


def _dot(a, b):
    return jnp.dot(a, b, preferred_element_type=F32)


ROW_CHUNK = 256


def _row_chunks(bb, t):
    if t >= ROW_CHUNK:
        return [(slice(b, b + 1), slice(r, r + ROW_CHUNK), slice(b * t + r, b * t + r + ROW_CHUNK))
                for b in range(bb) for r in range(0, t, ROW_CHUNK)]
    nb = min(ROW_CHUNK // t, bb)
    return [(slice(b, b + nb), slice(0, t), slice(b * t, (b + nb) * t)) for b in range(0, bb, nb)]


def _ada_kernel(c_ref, w_ref, b_ref, o_ref):
    a = jax.nn.silu(c_ref[...]).astype(BF16)
    o_ref[...] = _dot(a, w_ref[...].astype(BF16)) + b_ref[...]


def _ada(c, w_ada, b_ada, tn=1024):
    n_rows, d = c.shape
    n_out = w_ada.shape[1]
    return pl.pallas_call(
        _ada_kernel,
        grid=(n_out // tn,),
        in_specs=[pl.BlockSpec((n_rows, d), lambda j: (0, 0)),
                  pl.BlockSpec((d, tn), lambda j: (0, j)),
                  pl.BlockSpec((1, tn), lambda j: (0, j))],
        out_specs=pl.BlockSpec((n_rows, tn), lambda j: (0, j)),
        out_shape=jax.ShapeDtypeStruct((n_rows, n_out), F32),
        compiler_params=_params(1, 40 * 1024 * 1024),
        name="ada",
    )(c, w_ada, b_ada.reshape(1, n_out))


def _band_bias_kernel(tab_ref, o_ref):
    h = pl.program_id(0)
    t = BAND_ROWS
    w = 2 * t
    kidx = lax.broadcasted_iota(jnp.int32, (1, w), 1)
    d = jnp.where(kidx < t, kidx, kidx - w)
    idx_cur = jnp.clip(d, -MAX_REL, MAX_REL) + MAX_REL
    idx_prev = jnp.clip(d - t, -MAX_REL, MAX_REL) + MAX_REL

    def fill(r, rows):
        rc, rp = rows
        val = tab_ref[h, r] * LOG2E
        return jnp.where(idx_cur == r, val, rc), jnp.where(idx_prev == r, val, rp)

    rc, rp = lax.fori_loop(0, N_REL, fill, (jnp.zeros((1, w), F32), jnp.zeros((1, w), F32)))
    n_q = o_ref.shape[2]
    qrow = lax.broadcasted_iota(jnp.int32, (n_q, w), 0)

    def toeplitz(row):
        x = jnp.broadcast_to(row, (n_q, w))
        for bit in range(n_q.bit_length() - 1):
            x = jnp.where(((qrow >> bit) & 1) == 1, pltpu.roll(x, 1 << bit, 1), x)
        return x[:, :t]

    qc = lax.broadcasted_iota(jnp.int32, (n_q, t), 0) // CHUNK
    kc = lax.broadcasted_iota(jnp.int32, (n_q, t), 1) // CHUNK
    o_ref[0, 1] = jnp.where(kc <= qc, toeplitz(rc), NEG)
    o_ref[0, 0] = jnp.where(kc >= qc, toeplitz(rp), NEG)


def _band_bias(table, n_q):
    t = BAND_ROWS
    assert n_q & (n_q - 1) == 0 and n_q <= t
    return pl.pallas_call(
        _band_bias_kernel,
        grid=(N_HEADS,),
        in_specs=[pl.BlockSpec(memory_space=pltpu.SMEM)],
        out_specs=pl.BlockSpec((1, 2, n_q, t), lambda h: (h, 0, 0, 0)),
        out_shape=jax.ShapeDtypeStruct((N_HEADS, 2, n_q, t), F32),
        compiler_params=_params(1, 40 * 1024 * 1024),
        name="band_bias",
    )(table)


def _in_proj_kernel(x_ref, ada_ref, wfox_ref, wband_ref, wf_ref, bf_ref,
                    qa_ref, ka_ref, va_ref, qb_ref, kb_ref, vb_ref, kbt_ref, vbt_ref, lf_ref, u_ref,
                    *, tiles_per_out):
    j = pl.program_id(2)
    bb, t, d = x_ref.shape
    tn = wfox_ref.shape[1]
    tail = kbt_ref.shape[1]
    in_tail_block = pl.program_id(1) == pl.num_programs(1) - 1

    @pl.when(j == 0)
    def _():
        for bs, rs, flat in _row_chunks(bb, t):
            x = x_ref[bs, rs, :]
            u = _norm(x) * (1.0 + ada_ref[bs, 1:2, :]) + ada_ref[bs, 0:1, :]
            u = u.reshape(flat.stop - flat.start, d).astype(BF16)
            u_ref[flat, :] = u
            qa_ref[bs, rs, :] = (_dot(u, wfox_ref[...]) * Q_SCALE).astype(qa_ref.dtype).reshape(x.shape[:2] + (tn,))
        f = _dot_nt(wf_ref[...], u_ref[...])
        lf_ref[0] = jax.nn.log_sigmoid(f[:N_HEADS] + bf_ref[...])

    outs = ((qa_ref, Q_SCALE, None), (ka_ref, None, None), (va_ref, None, None),
            (qb_ref, Q_SCALE, None), (kb_ref, None, kbt_ref), (vb_ref, None, vbt_ref))
    for o, (ref, scale, tail_ref) in enumerate(outs):
        @pl.when(jnp.logical_and(j // tiles_per_out == o, j > 0))
        def _(ref=ref, scale=scale, tail_ref=tail_ref, w_ref=wfox_ref if o < 3 else wband_ref):
            acc = _dot(u_ref[...], w_ref[...])
            val = acc if scale is None else acc * scale
            ref[...] = val.astype(ref.dtype).reshape(bb, t, tn)
            if tail_ref is not None:
                @pl.when(in_tail_block)
                def _():
                    tail_ref[...] = acc.reshape(bb, t, tn)[:, t - tail:, :]


def _in_proj(x, ada, w_fox, w_band, wf_t, b_forget, *, bb, t, tn=512):
    n_b, s, d = x.shape
    m = bb * t
    tiles_per_out = GROUP_WIDTH // tn
    n_j = 6 * tiles_per_out

    n_i = s // t
    n_blocks = (n_b // bb) * n_i

    def out_spec(o):
        def index(b, i, j):
            rel = j - o * tiles_per_out
            lin = b * n_i + i
            move = jnp.logical_and(rel >= tiles_per_out, lin < n_blocks - 1)
            lin = lin + move.astype(jnp.int32)
            col = jnp.where(move, 0, jnp.clip(rel, 0, tiles_per_out - 1))
            return lin // n_i, lin % n_i, col
        return pl.BlockSpec((bb, t, tn), index)

    tail = min(BAND_ROWS, s)

    def tail_spec(o):
        def index(b, i, j):
            col = jnp.where(i == n_i - 1, jnp.clip(j - o * tiles_per_out, 0, tiles_per_out - 1), 0)
            return b, 0, col
        return pl.BlockSpec((bb, tail, tn), index)

    tok = lambda dt: jax.ShapeDtypeStruct((n_b, s, GROUP_WIDTH), dt)
    tail_shape = jax.ShapeDtypeStruct((n_b, tail, GROUP_WIDTH), F32)
    return pl.pallas_call(
        functools.partial(_in_proj_kernel, tiles_per_out=tiles_per_out),
        grid=(n_b // bb, s // t, n_j),
        in_specs=[pl.BlockSpec((bb, t, d), lambda b, i, j: (b, i, 0)),
                  pl.BlockSpec((bb, 6, d), lambda b, i, j: (b, 0, 0)),
                  pl.BlockSpec((d, tn), lambda b, i, j: (0, jnp.minimum(j, n_j // 2 - 1))),
                  pl.BlockSpec((d, tn), lambda b, i, j: (0, jnp.maximum(j - n_j // 2, 0))),
                  pl.BlockSpec(wf_t.shape, lambda b, i, j: (0, 0)),
                  pl.BlockSpec((N_HEADS, 1), lambda b, i, j: (0, 0))],
        out_specs=[out_spec(o) for o in range(6)] + [tail_spec(4), tail_spec(5)]
        + [pl.BlockSpec((1, N_HEADS, m), lambda b, i, j: (b, 0, i))],
        out_shape=[tok(BF16), tok(F32), tok(F32), tok(BF16), tok(BF16), tok(BF16), tail_shape, tail_shape,
                   jax.ShapeDtypeStruct((n_b // bb, N_HEADS, (s // t) * m), F32)],
        scratch_shapes=[pltpu.VMEM((m, d), BF16)],
        compiler_params=_params(3, VMEM_LIMIT_BYTES),
        name="in_proj",
    )(x, ada, w_fox, w_band, wf_t, b_forget.reshape(N_HEADS, 1))


def _cumsum_kernel(x_ref, o_ref):
    x = x_ref[0]
    n = x.shape[1]
    lane = lax.broadcasted_iota(jnp.int32, x.shape, 1)
    shift = 1
    while shift < n:
        x = x + jnp.where(lane >= shift, pltpu.roll(x, shift, 1), 0.0)
        shift *= 2
    o_ref[0] = x * LOG2E


def _cumsum_lanes(x):
    n_b, h, n = x.shape
    spec = pl.BlockSpec((1, h, n), lambda b: (b, 0, 0))
    return pl.pallas_call(
        _cumsum_kernel, grid=(n_b,), in_specs=[spec], out_specs=spec,
        out_shape=jax.ShapeDtypeStruct(x.shape, F32),
        compiler_params=_params(1), name="cumsum",
    )(x)


def _fox_prompt_kernel(q_ref, k_ref, v_ref, frow_ref, o_ref,
                       kb_ref, vb_ref, fq_ref, m_ref, l_ref, acc_ref, *, blk, strip, hp):
    qi = pl.program_id(2)
    lanes = D_HEAD
    n_col = blk // lanes
    head_lanes = [slice(hh * lanes, (hh + 1) * lanes) for hh in range(hp)]

    @pl.when(qi == 0)
    def _():
        for hh in range(hp):
            kb_ref[hh] = k_ref[0, :, head_lanes[hh]].astype(BF16)
            vb_ref[hh] = v_ref[0, :, head_lanes[hh]].astype(BF16)

    on_diag = (lax.broadcasted_iota(jnp.int32, (blk, blk), 0) == lax.broadcasted_iota(jnp.int32, (blk, blk), 1))
    for hh in range(hp):
        f_col = jnp.sum(jnp.where(on_diag, frow_ref[hh, qi], 0.0), axis=1, keepdims=True)
        fq_ref[hh] = jnp.broadcast_to(f_col, (blk, lanes))
    m_ref[...] = jnp.full(m_ref.shape, NEG, F32)
    l_ref[...] = jnp.zeros(l_ref.shape, F32)
    acc_ref[...] = jnp.zeros(acc_ref.shape, F32)

    def scores(hh, j):
        start = pl.multiple_of(j * blk, blk)
        return _dot_nt(q_ref[0, :, head_lanes[hh]], kb_ref[hh, pl.ds(start, blk), :])

    def softmax(hh, j, s_all, masked):
        f_keys = frow_ref[hh, j]
        p_strips = []
        for r in range(blk // strip):
            rows = slice(r * strip, (r + 1) * strip)
            fq = fq_ref[hh, rows]
            cols = []
            for cc in range(n_col):
                k0 = cc * lanes
                if masked and k0 >= (r + 1) * strip:
                    continue
                s = s_all[rows, k0:k0 + lanes] + fq - f_keys[:, k0:k0 + lanes]
                if masked and k0 + lanes - 1 > r * strip:
                    row = lax.broadcasted_iota(jnp.int32, s.shape, 0) + r * strip
                    col = lax.broadcasted_iota(jnp.int32, s.shape, 1) + k0
                    s = jnp.where(col <= row, s, NEG)
                cols.append(s)
            m_old = m_ref[hh, rows]
            m_blk = functools.reduce(jnp.maximum, cols)
            m_new = jnp.maximum(m_old, jnp.broadcast_to(jnp.max(m_blk, axis=1, keepdims=True), m_old.shape))
            ps = [jnp.exp2(s - m_new) for s in cols]
            a = jnp.exp2(m_old - m_new)
            l_ref[hh, rows] = a * l_ref[hh, rows] + functools.reduce(jnp.add, ps)
            m_ref[hh, rows] = m_new
            acc_ref[hh, rows] = a * acc_ref[hh, rows]
            ps = [p.astype(BF16) for p in ps] + [jnp.zeros((strip, lanes), BF16)] * (n_col - len(ps))
            p_strips.append(jnp.concatenate(ps, axis=1))
        return jnp.concatenate(p_strips, axis=0)

    def values(hh, j, p):
        start = pl.multiple_of(j * blk, blk)
        acc_ref[hh] += _dot(p, vb_ref[hh, pl.ds(start, blk), :])

    def block_step(j, masked):
        s_next = scores(0, j)
        for hh in range(hp):
            s_cur = s_next
            if hh + 1 < hp:
                s_next = scores(hh + 1, j)
            values(hh, j, softmax(hh, j, s_cur, masked))

    def loop_body(j, carry):
        block_step(j, False)
        return carry

    lax.fori_loop(0, qi, loop_body, 0)
    block_step(qi, True)
    for hh in range(hp):
        l = jnp.sum(l_ref[hh], axis=1, keepdims=True)
        o_ref[0, :, head_lanes[hh]] = (acc_ref[hh] / l).astype(o_ref.dtype)


def _fox_prompt(q, k, v, f, *, blk=512, strip=32, hp=4):
    n_b, s, _ = q.shape
    n_blk = s // blk
    n_hg = N_HEADS // hp
    frow = f.reshape(n_b * N_HEADS, n_blk, 1, blk)
    kv_spec = pl.BlockSpec((1, s, hp * D_HEAD), lambda b, h, i: (b, 0, h))
    q_spec = pl.BlockSpec((1, blk, hp * D_HEAD), lambda b, h, i: (b, i, h))
    state = pltpu.VMEM((hp, blk, D_HEAD), F32)
    return pl.pallas_call(
        functools.partial(_fox_prompt_kernel, blk=blk, strip=strip, hp=hp),
        grid=(n_b, n_hg, n_blk),
        in_specs=[q_spec, kv_spec, kv_spec,
                  pl.BlockSpec((hp, n_blk, 1, blk), lambda b, h, i: (b * n_hg + h, 0, 0, 0))],
        out_specs=q_spec,
        out_shape=jax.ShapeDtypeStruct((n_b, s, GROUP_WIDTH), BF16),
        scratch_shapes=[pltpu.VMEM((hp, s, D_HEAD), BF16), pltpu.VMEM((hp, s, D_HEAD), BF16),
                        state, state, state, state],
        compiler_params=_params(3, VMEM_LIMIT_BYTES),
        name="fox_prompt",
    )(q, k, v, frow)


def _band_prompt_kernel(q_ref, k_ref, v_ref, bias_ref, o_ref, kb_ref, vb_ref, *, tb, nq, strip, hp):
    i = pl.program_id(2)
    past = BAND_ROWS
    lanes = D_HEAD
    win = past + tb
    head_lanes = [slice(hh * lanes, (hh + 1) * lanes) for hh in range(hp)]

    @pl.when(i == 0)
    def _():
        for hh in range(hp):
            kb_ref[hh, :past, :] = jnp.zeros((past, D_HEAD), BF16)
            vb_ref[hh, :past, :] = jnp.zeros((past, D_HEAD), BF16)
            kb_ref[hh, past:, :] = k_ref[0, :, head_lanes[hh]]
            vb_ref[hh, past:, :] = v_ref[0, :, head_lanes[hh]]

    chains = [(hh, qq) for qq in range(nq) for hh in range(hp)]

    def block_start(qq):
        return pl.multiple_of((i * nq + qq) * tb, tb)

    def scores(chain):
        hh, qq = chain
        q = q_ref[0, qq * tb:(qq + 1) * tb, head_lanes[hh]]
        return _dot_nt(q, kb_ref[hh, pl.ds(block_start(qq), win), :])

    def softmax(chain, s_all):
        hh, qq = chain
        first_valid = past - block_start(qq)
        p_strips, l_strips = [], []
        for r in range(tb // strip):
            rows = slice(r * strip, (r + 1) * strip)
            q_chunk = (r * strip) // CHUNK
            cols = {}
            for cc in range(win // lanes):
                k0 = cc * lanes
                if k0 < past:
                    if (k0 + lanes - 1) // CHUNK < q_chunk:
                        continue
                    s = s_all[rows, k0:k0 + lanes] + bias_ref[hh, 0, rows, k0:k0 + lanes]
                    col = lax.broadcasted_iota(jnp.int32, s.shape, 1) + k0
                    s = jnp.where(col >= first_valid, s, NEG)
                else:
                    if (k0 - past) // CHUNK > q_chunk:
                        continue
                    s = s_all[rows, k0:k0 + lanes] + bias_ref[hh, 1, rows, k0 - past:k0 - past + lanes]
                cols[cc] = s
            m_blk = functools.reduce(jnp.maximum, cols.values())
            m = jnp.broadcast_to(jnp.max(m_blk, axis=1, keepdims=True), m_blk.shape)
            ps = {cc: jnp.exp2(s - m) for cc, s in cols.items()}
            l_strips.append(functools.reduce(jnp.add, ps.values()))
            zero = jnp.zeros((strip, lanes), BF16)
            p_strips.append(jnp.concatenate(
                [ps[cc].astype(BF16) if cc in ps else zero for cc in range(win // lanes)], axis=1))
        return jnp.concatenate(p_strips, axis=0), jnp.concatenate(l_strips, axis=0)

    def values(chain, p, l_part):
        hh, qq = chain
        acc = _dot(p, vb_ref[hh, pl.ds(block_start(qq), win), :])
        l = jnp.sum(l_part, axis=1, keepdims=True)
        o_ref[0, qq * tb:(qq + 1) * tb, head_lanes[hh]] = (acc / l).astype(o_ref.dtype)

    s_next = scores(chains[0])
    for n, chain in enumerate(chains):
        s_cur = s_next
        if n + 1 < len(chains):
            s_next = scores(chains[n + 1])
        values(chain, *softmax(chain, s_cur))


def _band_prompt(q, k, v, bias, *, tb=BAND_Q_BLOCK, nq=2, strip=32, hp=4):
    n_b, s, _ = q.shape
    kv_spec = pl.BlockSpec((1, s, hp * D_HEAD), lambda h, b, i: (b, 0, h))
    q_spec = pl.BlockSpec((1, nq * tb, hp * D_HEAD), lambda h, b, i: (b, i, h))
    kv_scratch = pltpu.VMEM((hp, BAND_ROWS + s, D_HEAD), BF16)
    return pl.pallas_call(
        functools.partial(_band_prompt_kernel, tb=tb, nq=nq, strip=strip, hp=hp),
        grid=(N_HEADS // hp, n_b, s // (nq * tb)),
        in_specs=[q_spec, kv_spec, kv_spec,
                  pl.BlockSpec((hp, 2, tb, BAND_ROWS), lambda h, b, i: (h, 0, 0, 0))],
        out_specs=q_spec,
        out_shape=jax.ShapeDtypeStruct((n_b, s, GROUP_WIDTH), BF16),
        scratch_shapes=[kv_scratch, kv_scratch],
        compiler_params=_params(3, VMEM_LIMIT_BYTES),
        name="band_prompt",
    )(q, k, v, bias)


def _fox_sample_kernel(q_ref, kc_ref, vc_ref, kn_ref, vn_ref, fc_ref, fn_ref, fq_ref, o_ref,
                       m_ref, l_ref, acc_ref, *, tc):
    c = pl.program_id(1)
    n_new = q_ref.shape[1]
    lanes = D_HEAD

    @pl.when(c == 0)
    def _():
        m_ref[...] = jnp.full(m_ref.shape, NEG, F32)
        l_ref[...] = jnp.zeros(l_ref.shape, F32)
        acc_ref[...] = jnp.zeros(acc_ref.shape, F32)

    def update(h, s, v):
        m_old = m_ref[h][:, :1]
        m_new = jnp.maximum(m_old, jnp.max(s, axis=1, keepdims=True))
        p = jnp.exp2(s - m_new)
        a = jnp.exp2(m_old - m_new)
        l_ref[h] = a * l_ref[h] + jnp.sum(p, axis=1, keepdims=True)
        m_ref[h] = jnp.broadcast_to(m_new, (n_new, lanes))
        acc_ref[h] = a * acc_ref[h] + _dot(p.astype(BF16), v)

    for h in range(N_HEADS):
        hl = slice(h * lanes, (h + 1) * lanes)
        rows = pl.ds(h, tc, stride=N_HEADS)
        s = (_dot_nt(q_ref[0, :, hl], kc_ref[0, rows, :].astype(BF16))
             + fq_ref[0, :, h:h + 1] - fc_ref[0, 0, h:h + 1, :])
        update(h, s, vc_ref[0, rows, :].astype(BF16))

    @pl.when(c == pl.num_programs(1) - 1)
    def _():
        for h in range(N_HEADS):
            hl = slice(h * lanes, (h + 1) * lanes)
            s = (_dot_nt(q_ref[0, :, hl], kn_ref[0, :, hl].astype(BF16))
                 + fq_ref[0, :, h:h + 1] - fn_ref[0, h:h + 1, :])
            row = lax.broadcasted_iota(jnp.int32, s.shape, 0)
            col = lax.broadcasted_iota(jnp.int32, s.shape, 1)
            update(h, jnp.where(col <= row, s, NEG), vn_ref[0, :, hl].astype(BF16))
            o_ref[0, :, hl] = (acc_ref[h] / l_ref[h]).astype(o_ref.dtype)


def _fox_sample(q, k_cache, v_cache, k_new, v_new, f_cache, f_new, f_q, *, tc=1024):
    n_b, n_new, _ = q.shape
    past = k_cache.shape[1] // N_HEADS
    n_chunks = past // tc
    f_cache = jnp.swapaxes(f_cache.reshape(n_b, N_HEADS, n_chunks, tc), 1, 2)
    new_spec = pl.BlockSpec((1, n_new, GROUP_WIDTH), lambda b, c: (b, 0, 0))
    cache_spec = pl.BlockSpec((1, tc * N_HEADS, D_HEAD), lambda b, c: (b, c, 0))
    state = pltpu.VMEM((N_HEADS, n_new, D_HEAD), F32)
    return pl.pallas_call(
        functools.partial(_fox_sample_kernel, tc=tc),
        grid=(n_b, n_chunks),
        in_specs=[new_spec, cache_spec, cache_spec, new_spec, new_spec,
                  pl.BlockSpec((1, 1, N_HEADS, tc), lambda b, c: (b, c, 0, 0)),
                  pl.BlockSpec((1, N_HEADS, n_new), lambda b, c: (b, 0, 0)),
                  pl.BlockSpec((1, n_new, N_HEADS), lambda b, c: (b, 0, 0))],
        out_specs=new_spec,
        out_shape=jax.ShapeDtypeStruct((n_b, n_new, GROUP_WIDTH), BF16),
        scratch_shapes=[state, state, state],
        compiler_params=_params(2, 40 * 1024 * 1024),
        name="fox_sample",
    )(q, k_cache, v_cache, k_new, v_new, f_cache, f_new, f_q)


def _band_sample_kernel(q_ref, kc_ref, vc_ref, kn_ref, vn_ref, bias_ref, o_ref):
    n_new = q_ref.shape[1]
    past = kc_ref.shape[1] // N_HEADS
    lanes = D_HEAD
    for h in range(N_HEADS):
        hl = slice(h * lanes, (h + 1) * lanes)
        rows = pl.ds(h, past, stride=N_HEADS)
        q = q_ref[0, :, hl]
        s_c = _dot_nt(q, kc_ref[0, rows, :].astype(BF16)) + bias_ref[h, 0]
        s_n = _dot_nt(q, kn_ref[0, :, hl].astype(BF16)) + bias_ref[h, 1][:, :n_new]
        m = jnp.maximum(jnp.max(s_c, axis=1, keepdims=True), jnp.max(s_n, axis=1, keepdims=True))
        p_c = jnp.exp2(s_c - m)
        p_n = jnp.exp2(s_n - m)
        l = jnp.sum(p_c, axis=1, keepdims=True) + jnp.sum(p_n, axis=1, keepdims=True)
        acc = (_dot(p_c.astype(BF16), vc_ref[0, rows, :].astype(BF16))
               + _dot(p_n.astype(BF16), vn_ref[0, :, hl].astype(BF16)))
        o_ref[0, :, hl] = (acc / l).astype(o_ref.dtype)


def _band_sample(q, k_cache, v_cache, k_new, v_new, bias):
    n_b, n_new, _ = q.shape
    new_spec = pl.BlockSpec((1, n_new, GROUP_WIDTH), lambda b: (b, 0, 0))
    cache_spec = pl.BlockSpec((1,) + k_cache.shape[1:], lambda b: (b, 0, 0))
    return pl.pallas_call(
        _band_sample_kernel,
        grid=(n_b,),
        in_specs=[new_spec, cache_spec, cache_spec, new_spec, new_spec,
                  pl.BlockSpec((N_HEADS, 2, n_new, BAND_ROWS), lambda b: (0, 0, 0, 0))],
        out_specs=new_spec,
        out_shape=jax.ShapeDtypeStruct((n_b, n_new, GROUP_WIDTH), BF16),
        compiler_params=_params(1, 40 * 1024 * 1024),
        name="band_sample",
    )(q, k_cache, v_cache, k_new, v_new, bias)


def _out_proj_kernel(fox_ref, band_ref, x_ref, ada_ref, w_ref, g_ref, b_ref, o_ref):
    bb, t, d = x_ref.shape
    gw = fox_ref.shape[2]
    for bs, rs, flat in _row_chunks(bb, t):
        x = x_ref[bs, rs, :]
        rows = flat.stop - flat.start
        y = (_dot(fox_ref[bs, rs, :].reshape(rows, gw), w_ref[:gw, :])
             + _dot(band_ref[bs, rs, :].reshape(rows, gw), w_ref[gw:, :]))
        z = ALPHA * x + ada_ref[bs, 2:3, :] * y.reshape(x.shape)
        o_ref[bs, rs, :] = _norm(z) * g_ref[...] + b_ref[...]


def _out_proj(fox, band, x, ada, w_out, g, b, *, bb, t):
    n_b, s, d = x.shape
    tok = lambda w: pl.BlockSpec((bb, t, w), lambda bi, i: (bi, i, 0))
    vec = pl.BlockSpec((1, d), lambda bi, i: (0, 0))
    return pl.pallas_call(
        _out_proj_kernel,
        grid=(n_b // bb, s // t),
        in_specs=[tok(GROUP_WIDTH), tok(GROUP_WIDTH), tok(d),
                  pl.BlockSpec((bb, 6, d), lambda bi, i: (bi, 0, 0)),
                  pl.BlockSpec(w_out.shape, lambda bi, i: (0, 0)), vec, vec],
        out_specs=tok(d),
        out_shape=jax.ShapeDtypeStruct(x.shape, F32),
        compiler_params=_params(2, VMEM_LIMIT_BYTES),
        name="out_proj",
    )(fox, band, x, ada, w_out, g.reshape(1, d), b.reshape(1, d))


def _mlp_kernel(x_ref, ada_ref, wu_ref, bu_ref, wd_ref, g_ref, b_ref, o_ref, u_ref):
    f = pl.program_id(2)
    last = pl.num_programs(2) - 1
    bb, t, d = x_ref.shape
    chunks = _row_chunks(bb, t)

    def hidden(u):
        hid = jnp.maximum(_dot(u, wu_ref[...]) + bu_ref[...], 0.0)
        return (hid * hid).astype(BF16)

    @pl.when(f == 0)
    def _():
        for bs, rs, flat in chunks:
            u = _norm(x_ref[bs, rs, :]) * (1.0 + ada_ref[bs, 4:5, :]) + ada_ref[bs, 3:4, :]
            u = u.reshape(flat.stop - flat.start, d).astype(BF16)
            u_ref[flat, :] = u
            o_ref[bs, rs, :] = _dot(hidden(u), wd_ref[...]).reshape(x_ref[bs, rs, :].shape)

    @pl.when(jnp.logical_and(f > 0, f < last))
    def _():
        o_ref[...] += _dot(hidden(u_ref[...]), wd_ref[...]).reshape(bb, t, d)

    @pl.when(f == last)
    def _():
        for bs, rs, flat in chunks:
            x = x_ref[bs, rs, :]
            y = o_ref[bs, rs, :] + _dot(hidden(u_ref[flat, :]), wd_ref[...]).reshape(x.shape)
            z = ALPHA * x + ada_ref[bs, 5:6, :] * y
            o_ref[bs, rs, :] = _norm(z) * g_ref[...] + b_ref[...]


def _mlp(x, ada, w_up, b_up, w_down, g, b, *, bb, t, tf=1024):
    n_b, s, d = x.shape
    d_ff = w_up.shape[1]
    assert d_ff // tf >= 2, "first and last hidden tiles are separate steps"
    tok = pl.BlockSpec((bb, t, d), lambda bi, i, f: (bi, i, 0))
    vec = pl.BlockSpec((1, d), lambda bi, i, f: (0, 0))
    return pl.pallas_call(
        _mlp_kernel,
        grid=(n_b // bb, s // t, d_ff // tf),
        in_specs=[tok, pl.BlockSpec((bb, 6, d), lambda bi, i, f: (bi, 0, 0)),
                  pl.BlockSpec((d, tf), lambda bi, i, f: (0, f)),
                  pl.BlockSpec((1, tf), lambda bi, i, f: (0, f)),
                  pl.BlockSpec((tf, d), lambda bi, i, f: (f, 0)), vec, vec],
        out_specs=tok,
        out_shape=jax.ShapeDtypeStruct(x.shape, F32),
        scratch_shapes=[pltpu.VMEM((bb * t, d), BF16)],
        compiler_params=_params(3, VMEM_LIMIT_BYTES),
        name="mlp",
    )(x, ada, w_up, b_up.reshape(1, d_ff), w_down, g.reshape(1, d), b.reshape(1, d))


def kernel(x_prompt, x_sample, cache_fox_k, cache_fox_v, cache_fox_logf, cache_band_k, cache_band_v,
           c_prompt, c_sample, w_ada, b_ada, w_in, b_forget, rel_bias, w_out, ln_mix_g, ln_mix_b,
           w_up, b_up, w_down, ln_mlp_g, ln_mlp_b):
    assert w_in.shape[0] == 1, "single-layer step"
    n_b, seq, d = x_prompt.shape
    n_bs, n_new, _ = x_sample.shape
    past = cache_fox_k.shape[2]
    gw = GROUP_WIDTH

    w = w_in[0]
    w_fox = w[:, :3 * gw].astype(BF16)
    w_band = w[:, 3 * gw + N_HEADS:].astype(BF16)
    wf_t = jnp.pad(w[:, 3 * gw:3 * gw + N_HEADS].T, ((0, 16 - N_HEADS), (0, 0))).astype(BF16)
    w_out_b = w_out[0].astype(BF16)
    w_up_b = w_up[0].astype(BF16)
    w_down_b = w_down[0].astype(BF16)

    ada = _ada(jnp.concatenate([c_prompt, c_sample], axis=0), w_ada[0], b_ada[0])
    ada = ada.reshape(n_b + n_bs, 6, d)
    ada_p, ada_s = ada[:n_b], ada[n_b:]
    bias = _band_bias(rel_bias[0], BAND_Q_BLOCK)

    qa, ka, va, qb, kb, vb, kb_tail, vb_tail, lft = _in_proj(
        x_prompt, ada_p, w_fox, w_band, wf_t, b_forget[0], bb=1, t=1024)
    f_p = _cumsum_lanes(lft)
    fox = _fox_prompt(qa, ka, va, f_p)
    band = _band_prompt(qb, kb, vb, bias)
    x1 = _out_proj(fox, band, x_prompt, ada_p, w_out_b, ln_mix_g[0], ln_mix_b[0], bb=1, t=512)
    y_p = _mlp(x1, ada_p, w_up_b, b_up[0], w_down_b, ln_mlp_g[0], ln_mlp_b[0], bb=1, t=1024)

    heads = lambda a: a.reshape(1, a.shape[0], a.shape[1], N_HEADS, D_HEAD)
    p_states = (heads(ka), heads(va), jnp.swapaxes(lft, 1, 2)[None], heads(kb_tail), heads(vb_tail))

    qa_s, ka_s, va_s, qb_s, kb_s, vb_s, kb_tail_s, vb_tail_s, lft_s = _in_proj(
        x_sample, ada_s, w_fox, w_band, wf_t, b_forget[0], bb=n_bs, t=n_new)
    logf_s = jnp.transpose(lft_s.reshape(N_HEADS, n_bs, n_new), (1, 0, 2))
    total = past + n_new
    padded = -(-total // 128) * 128
    logf_all = jnp.concatenate(
        [jnp.swapaxes(cache_fox_logf[0], 1, 2), logf_s, jnp.zeros((n_bs, N_HEADS, padded - total), F32)], axis=2)
    f_s = _cumsum_lanes(logf_all)
    f_new = f_s[:, :, past:total]
    rows_view = lambda cache: cache.reshape(n_bs, cache.shape[2] * N_HEADS, D_HEAD)
    fox_s = _fox_sample(qa_s, rows_view(cache_fox_k), rows_view(cache_fox_v), ka_s, va_s,
                        f_s[:, :, :past], f_new, jnp.swapaxes(f_new, 1, 2))
    band_s = _band_sample(qb_s, rows_view(cache_band_k), rows_view(cache_band_v), kb_s, vb_s, bias)
    x1_s = _out_proj(fox_s, band_s, x_sample, ada_s, w_out_b, ln_mix_g[0], ln_mix_b[0], bb=n_bs, t=n_new)
    y_s = _mlp(x1_s, ada_s, w_up_b, b_up[0], w_down_b, ln_mlp_g[0], ln_mlp_b[0], bb=n_bs, t=n_new)
    s_states = (heads(ka_s), heads(va_s), jnp.swapaxes(logf_s, 1, 2)[None],
                heads(kb_tail_s), heads(vb_tail_s))

    return (y_p, y_s) + p_states + s_states
```

```python
import functools
import math

import jax
import jax.numpy as jnp
from jax import lax
from jax.experimental import pallas as pl
from jax.experimental.pallas import tpu as pltpu

D_HEAD = 128
N_HEADS = 8
GROUP_WIDTH = N_HEADS * D_HEAD
CHUNK = 64
BAND_ROWS = 512
BAND_Q_BLOCK = 256
MAX_REL = 128
N_REL = 2 * MAX_REL + 1
ALPHA = 2.0 ** 0.25
LN_EPS = 1e-5
NEG = -1e30
LOG2E = math.log2(math.e)
Q_SCALE = D_HEAD ** -0.5 * LOG2E

VMEM_LIMIT_BYTES = 60 * 1024 * 1024

F32 = jnp.float32
BF16 = jnp.bfloat16


def _params(n_axes, vmem=None):
    return pltpu.CompilerParams(dimension_semantics=("arbitrary",) * n_axes, vmem_limit_bytes=vmem)


def _norm(x):
    mu = jnp.mean(x, axis=-1, keepdims=True)
    xc = x - mu
    var = jnp.mean(xc * xc, axis=-1, keepdims=True)
    return xc * lax.rsqrt(var + LN_EPS)


def _dot_nt(a, b):
    return lax.dot_general(a, b, (((1,), (1,)), ((), ())), preferred_element_type=F32)


def _dot(a, b):
    return jnp.dot(a, b, preferred_element_type=F32)


ROW_CHUNK = 256


def _row_chunks(bb, t):
    if t >= ROW_CHUNK:
        return [(slice(b, b + 1), slice(r, r + ROW_CHUNK), slice(b * t + r, b * t + r + ROW_CHUNK))
                for b in range(bb) for r in range(0, t, ROW_CHUNK)]
    nb = min(ROW_CHUNK // t, bb)
    return [(slice(b, b + nb), slice(0, t), slice(b * t, (b + nb) * t)) for b in range(0, bb, nb)]


def _ada_kernel(c_ref, w_ref, b_ref, o_ref):
    a = jax.nn.silu(c_ref[...]).astype(BF16)
    o_ref[...] = _dot(a, w_ref[...].astype(BF16)) + b_ref[...]


def _ada(c, w_ada, b_ada, tn=1024):
    n_rows, d = c.shape
    n_out = w_ada.shape[1]
    return pl.pallas_call(
        _ada_kernel,
        grid=(n_out // tn,),
        in_specs=[pl.BlockSpec((n_rows, d), lambda j: (0, 0)),
                  pl.BlockSpec((d, tn), lambda j: (0, j)),
                  pl.BlockSpec((1, tn), lambda j: (0, j))],
        out_specs=pl.BlockSpec((n_rows, tn), lambda j: (0, j)),
        out_shape=jax.ShapeDtypeStruct((n_rows, n_out), F32),
        compiler_params=_params(1, 40 * 1024 * 1024),
        name="ada",
    )(c, w_ada, b_ada.reshape(1, n_out))


def _band_bias_kernel(tab_ref, o_ref):
    h = pl.program_id(0)
    t = BAND_ROWS
    w = 2 * t
    kidx = lax.broadcasted_iota(jnp.int32, (1, w), 1)
    d = jnp.where(kidx < t, kidx, kidx - w)
    idx_cur = jnp.clip(d, -MAX_REL, MAX_REL) + MAX_REL
    idx_prev = jnp.clip(d - t, -MAX_REL, MAX_REL) + MAX_REL

    def fill(r, rows):
        rc, rp = rows
        val = tab_ref[h, r] * LOG2E
        return jnp.where(idx_cur == r, val, rc), jnp.where(idx_prev == r, val, rp)

    rc, rp = lax.fori_loop(0, N_REL, fill, (jnp.zeros((1, w), F32), jnp.zeros((1, w), F32)))
    n_q = o_ref.shape[2]
    qrow = lax.broadcasted_iota(jnp.int32, (n_q, w), 0)

    def toeplitz(row):
        x = jnp.broadcast_to(row, (n_q, w))
        for bit in range(n_q.bit_length() - 1):
            x = jnp.where(((qrow >> bit) & 1) == 1, pltpu.roll(x, 1 << bit, 1), x)
        return x[:, :t]

    qc = lax.broadcasted_iota(jnp.int32, (n_q, t), 0) // CHUNK
    kc = lax.broadcasted_iota(jnp.int32, (n_q, t), 1) // CHUNK
    o_ref[0, 1] = jnp.where(kc <= qc, toeplitz(rc), NEG)
    o_ref[0, 0] = jnp.where(kc >= qc, toeplitz(rp), NEG)


def _band_bias(table, n_q):
    t = BAND_ROWS
    assert n_q & (n_q - 1) == 0 and n_q <= t
    return pl.pallas_call(
        _band_bias_kernel,
        grid=(N_HEADS,),
        in_specs=[pl.BlockSpec(memory_space=pltpu.SMEM)],
        out_specs=pl.BlockSpec((1, 2, n_q, t), lambda h: (h, 0, 0, 0)),
        out_shape=jax.ShapeDtypeStruct((N_HEADS, 2, n_q, t), F32),
        compiler_params=_params(1, 40 * 1024 * 1024),
        name="band_bias",
    )(table)


def _in_proj_kernel(x_ref, ada_ref, wfox_ref, wband_ref, wf_ref, bf_ref,
                    qa_ref, ka_ref, va_ref, qb_ref, kb_ref, vb_ref, kbt_ref, vbt_ref, lf_ref, u_ref,
                    *, tiles_per_out):
    j = pl.program_id(2)
    bb, t, d = x_ref.shape
    tn = wfox_ref.shape[1]
    tail = kbt_ref.shape[1]
    in_tail_block = pl.program_id(1) == pl.num_programs(1) - 1

    @pl.when(j == 0)
    def _():
        for bs, rs, flat in _row_chunks(bb, t):
            x = x_ref[bs, rs, :]
            u = _norm(x) * (1.0 + ada_ref[bs, 1:2, :]) + ada_ref[bs, 0:1, :]
            u = u.reshape(flat.stop - flat.start, d).astype(BF16)
            u_ref[flat, :] = u
            qa_ref[bs, rs, :] = (_dot(u, wfox_ref[...]) * Q_SCALE).astype(qa_ref.dtype).reshape(x.shape[:2] + (tn,))
        f = _dot_nt(wf_ref[...], u_ref[...])
        lf_ref[0] = jax.nn.log_sigmoid(f[:N_HEADS] + bf_ref[...])

    outs = ((qa_ref, Q_SCALE, None), (ka_ref, None, None), (va_ref, None, None),
            (qb_ref, Q_SCALE, None), (kb_ref, None, kbt_ref), (vb_ref, None, vbt_ref))
    for o, (ref, scale, tail_ref) in enumerate(outs):
        @pl.when(jnp.logical_and(j // tiles_per_out == o, j > 0))
        def _(ref=ref, scale=scale, tail_ref=tail_ref, w_ref=wfox_ref if o < 3 else wband_ref):
            acc = _dot(u_ref[...], w_ref[...])
            val = acc if scale is None else acc * scale
            ref[...] = val.astype(ref.dtype).reshape(bb, t, tn)
            if tail_ref is not None:
                @pl.when(in_tail_block)
                def _():
                    tail_ref[...] = acc.reshape(bb, t, tn)[:, t - tail:, :]


def _in_proj(x, ada, w_fox, w_band, wf_t, b_forget, *, bb, t, tn=512):
    n_b, s, d = x.shape
    m = bb * t
    tiles_per_out = GROUP_WIDTH // tn
    n_j = 6 * tiles_per_out

    n_i = s // t

    def out_spec(o):
        def index(b, i, j):
            return b, i, jnp.clip(j - o * tiles_per_out, 0, tiles_per_out - 1)
        return pl.BlockSpec((bb, t, tn), index)

    tail = min(BAND_ROWS, s)

    def tail_spec(o):
        def index(b, i, j):
            col = jnp.where(i == n_i - 1, jnp.clip(j - o * tiles_per_out, 0, tiles_per_out - 1), 0)
            return b, 0, col
        return pl.BlockSpec((bb, tail, tn), index)

    tok = lambda dt: jax.ShapeDtypeStruct((n_b, s, GROUP_WIDTH), dt)
    tail_shape = jax.ShapeDtypeStruct((n_b, tail, GROUP_WIDTH), F32)
    return pl.pallas_call(
        functools.partial(_in_proj_kernel, tiles_per_out=tiles_per_out),
        grid=(n_b // bb, s // t, n_j),
        in_specs=[pl.BlockSpec((bb, t, d), lambda b, i, j: (b, i, 0)),
                  pl.BlockSpec((bb, 6, d), lambda b, i, j: (b, 0, 0)),
                  pl.BlockSpec((d, tn), lambda b, i, j: (0, jnp.minimum(j, n_j // 2 - 1))),
                  pl.BlockSpec((d, tn), lambda b, i, j: (0, jnp.maximum(j - n_j // 2, 0))),
                  pl.BlockSpec(wf_t.shape, lambda b, i, j: (0, 0)),
                  pl.BlockSpec((N_HEADS, 1), lambda b, i, j: (0, 0))],
        out_specs=[out_spec(o) for o in range(6)] + [tail_spec(4), tail_spec(5)]
        + [pl.BlockSpec((1, N_HEADS, m), lambda b, i, j: (b, 0, i))],
        out_shape=[tok(BF16), tok(F32), tok(F32), tok(BF16), tok(BF16), tok(BF16), tail_shape, tail_shape,
                   jax.ShapeDtypeStruct((n_b // bb, N_HEADS, (s // t) * m), F32)],
        scratch_shapes=[pltpu.VMEM((m, d), BF16)],
        compiler_params=_params(3, VMEM_LIMIT_BYTES),
        name="in_proj",
    )(x, ada, w_fox, w_band, wf_t, b_forget.reshape(N_HEADS, 1))


def _cumsum_kernel(x_ref, o_ref):
    x = x_ref[0]
    n = x.shape[1]
    lane = lax.broadcasted_iota(jnp.int32, x.shape, 1)
    shift = 1
    while shift < n:
        x = x + jnp.where(lane >= shift, pltpu.roll(x, shift, 1), 0.0)
        shift *= 2
    o_ref[0] = x * LOG2E


def _cumsum_lanes(x):
    n_b, h, n = x.shape
    spec = pl.BlockSpec((1, h, n), lambda b: (b, 0, 0))
    return pl.pallas_call(
        _cumsum_kernel, grid=(n_b,), in_specs=[spec], out_specs=spec,
        out_shape=jax.ShapeDtypeStruct(x.shape, F32),
        compiler_params=_params(1), name="cumsum",
    )(x)


def _fox_prompt_kernel(q_ref, k_ref, v_ref, frow_ref, o_ref,
                       kb_ref, vb_ref, fq_ref, m_ref, l_ref, acc_ref, *, blk, strip, hp):
    qi = pl.program_id(2)
    lanes = D_HEAD
    n_col = blk // lanes
    head_lanes = [slice(hh * lanes, (hh + 1) * lanes) for hh in range(hp)]

    @pl.when(qi == 0)
    def _():
        for hh in range(hp):
            kb_ref[hh] = k_ref[0, :, head_lanes[hh]].astype(BF16)
            vb_ref[hh] = v_ref[0, :, head_lanes[hh]].astype(BF16)

    on_diag = (lax.broadcasted_iota(jnp.int32, (blk, blk), 0) == lax.broadcasted_iota(jnp.int32, (blk, blk), 1))
    for hh in range(hp):
        f_col = jnp.sum(jnp.where(on_diag, frow_ref[hh, qi], 0.0), axis=1, keepdims=True)
        fq_ref[hh] = jnp.broadcast_to(f_col, (blk, lanes))
    m_ref[...] = jnp.full(m_ref.shape, NEG, F32)
    l_ref[...] = jnp.zeros(l_ref.shape, F32)
    acc_ref[...] = jnp.zeros(acc_ref.shape, F32)

    def scores(hh, j):
        start = pl.multiple_of(j * blk, blk)
        return _dot_nt(q_ref[0, :, head_lanes[hh]], kb_ref[hh, pl.ds(start, blk), :])

    def softmax(hh, j, s_all, masked):
        f_keys = frow_ref[hh, j]
        p_strips = []
        for r in range(blk // strip):
            rows = slice(r * strip, (r + 1) * strip)
            fq = fq_ref[hh, rows]
            cols = []
            for cc in range(n_col):
                k0 = cc * lanes
                if masked and k0 >= (r + 1) * strip:
                    continue
                s = s_all[rows, k0:k0 + lanes] + fq - f_keys[:, k0:k0 + lanes]
                if masked and k0 + lanes - 1 > r * strip:
                    row = lax.broadcasted_iota(jnp.int32, s.shape, 0) + r * strip
                    col = lax.broadcasted_iota(jnp.int32, s.shape, 1) + k0
                    s = jnp.where(col <= row, s, NEG)
                cols.append(s)
            m_old = m_ref[hh, rows]
            m_blk = functools.reduce(jnp.maximum, cols)
            m_new = jnp.maximum(m_old, jnp.broadcast_to(jnp.max(m_blk, axis=1, keepdims=True), m_old.shape))
            ps = [jnp.exp2(s - m_new) for s in cols]
            a = jnp.exp2(m_old - m_new)
            l_ref[hh, rows] = a * l_ref[hh, rows] + functools.reduce(jnp.add, ps)
            m_ref[hh, rows] = m_new
            acc_ref[hh, rows] = a * acc_ref[hh, rows]
            ps = [p.astype(BF16) for p in ps] + [jnp.zeros((strip, lanes), BF16)] * (n_col - len(ps))
            p_strips.append(jnp.concatenate(ps, axis=1))
        return jnp.concatenate(p_strips, axis=0)

    def values(hh, j, p):
        start = pl.multiple_of(j * blk, blk)
        acc_ref[hh] += _dot(p, vb_ref[hh, pl.ds(start, blk), :])

    def block_step(j, masked):
        s_next = scores(0, j)
        for hh in range(hp):
            s_cur = s_next
            if hh + 1 < hp:
                s_next = scores(hh + 1, j)
            values(hh, j, softmax(hh, j, s_cur, masked))

    def loop_body(j, carry):
        block_step(j, False)
        return carry

    lax.fori_loop(0, qi, loop_body, 0)
    block_step(qi, True)
    for hh in range(hp):
        l = jnp.sum(l_ref[hh], axis=1, keepdims=True)
        o_ref[0, :, head_lanes[hh]] = (acc_ref[hh] / l).astype(o_ref.dtype)


def _fox_prompt(q, k, v, f, *, blk=512, strip=32, hp=4):
    n_b, s, _ = q.shape
    n_blk = s // blk
    n_hg = N_HEADS // hp
    frow = f.reshape(n_b * N_HEADS, n_blk, 1, blk)
    kv_spec = pl.BlockSpec((1, s, hp * D_HEAD), lambda b, h, i: (b, 0, h))
    q_spec = pl.BlockSpec((1, blk, hp * D_HEAD), lambda b, h, i: (b, i, h))
    state = pltpu.VMEM((hp, blk, D_HEAD), F32)
    return pl.pallas_call(
        functools.partial(_fox_prompt_kernel, blk=blk, strip=strip, hp=hp),
        grid=(n_b, n_hg, n_blk),
        in_specs=[q_spec, kv_spec, kv_spec,
                  pl.BlockSpec((hp, n_blk, 1, blk), lambda b, h, i: (b * n_hg + h, 0, 0, 0))],
        out_specs=q_spec,
        out_shape=jax.ShapeDtypeStruct((n_b, s, GROUP_WIDTH), BF16),
        scratch_shapes=[pltpu.VMEM((hp, s, D_HEAD), BF16), pltpu.VMEM((hp, s, D_HEAD), BF16),
                        state, state, state, state],
        compiler_params=_params(3, VMEM_LIMIT_BYTES),
        name="fox_prompt",
    )(q, k, v, frow)


def _band_prompt_kernel(q_ref, k_ref, v_ref, bias_ref, o_ref, kb_ref, vb_ref, *, tb, nq, strip, hp):
    i = pl.program_id(2)
    past = BAND_ROWS
    lanes = D_HEAD
    win = past + tb
    head_lanes = [slice(hh * lanes, (hh + 1) * lanes) for hh in range(hp)]

    @pl.when(i == 0)
    def _():
        for hh in range(hp):
            kb_ref[hh, :past, :] = jnp.zeros((past, D_HEAD), BF16)
            vb_ref[hh, :past, :] = jnp.zeros((past, D_HEAD), BF16)
            kb_ref[hh, past:, :] = k_ref[0, :, head_lanes[hh]]
            vb_ref[hh, past:, :] = v_ref[0, :, head_lanes[hh]]

    chains = [(hh, qq) for qq in range(nq) for hh in range(hp)]

    def block_start(qq):
        return pl.multiple_of((i * nq + qq) * tb, tb)

    def scores(chain):
        hh, qq = chain
        q = q_ref[0, qq * tb:(qq + 1) * tb, head_lanes[hh]]
        return _dot_nt(q, kb_ref[hh, pl.ds(block_start(qq), win), :])

    def softmax(chain, s_all):
        hh, qq = chain
        first_valid = past - block_start(qq)
        p_strips, l_strips = [], []
        for r in range(tb // strip):
            rows = slice(r * strip, (r + 1) * strip)
            q_chunk = (r * strip) // CHUNK
            cols = {}
            for cc in range(win // lanes):
                k0 = cc * lanes
                if k0 < past:
                    if (k0 + lanes - 1) // CHUNK < q_chunk:
                        continue
                    s = s_all[rows, k0:k0 + lanes] + bias_ref[hh, 0, rows, k0:k0 + lanes]
                    col = lax.broadcasted_iota(jnp.int32, s.shape, 1) + k0
                    s = jnp.where(col >= first_valid, s, NEG)
                else:
                    if (k0 - past) // CHUNK > q_chunk:
                        continue
                    s = s_all[rows, k0:k0 + lanes] + bias_ref[hh, 1, rows, k0 - past:k0 - past + lanes]
                cols[cc] = s
            m_blk = functools.reduce(jnp.maximum, cols.values())
            m = jnp.broadcast_to(jnp.max(m_blk, axis=1, keepdims=True), m_blk.shape)
            ps = {cc: jnp.exp2(s - m) for cc, s in cols.items()}
            l_strips.append(functools.reduce(jnp.add, ps.values()))
            zero = jnp.zeros((strip, lanes), BF16)
            p_strips.append(jnp.concatenate(
                [ps[cc].astype(BF16) if cc in ps else zero for cc in range(win // lanes)], axis=1))
        return jnp.concatenate(p_strips, axis=0), jnp.concatenate(l_strips, axis=0)

    def values(chain, p, l_part):
        hh, qq = chain
        acc = _dot(p, vb_ref[hh, pl.ds(block_start(qq), win), :])
        l = jnp.sum(l_part, axis=1, keepdims=True)
        o_ref[0, qq * tb:(qq + 1) * tb, head_lanes[hh]] = (acc / l).astype(o_ref.dtype)

    s_next = scores(chains[0])
    for n, chain in enumerate(chains):
        s_cur = s_next
        if n + 1 < len(chains):
            s_next = scores(chains[n + 1])
        values(chain, *softmax(chain, s_cur))


def _band_prompt(q, k, v, bias, *, tb=BAND_Q_BLOCK, nq=2, strip=32, hp=4):
    n_b, s, _ = q.shape
    kv_spec = pl.BlockSpec((1, s, hp * D_HEAD), lambda h, b, i: (b, 0, h))
    q_spec = pl.BlockSpec((1, nq * tb, hp * D_HEAD), lambda h, b, i: (b, i, h))
    kv_scratch = pltpu.VMEM((hp, BAND_ROWS + s, D_HEAD), BF16)
    return pl.pallas_call(
        functools.partial(_band_prompt_kernel, tb=tb, nq=nq, strip=strip, hp=hp),
        grid=(N_HEADS // hp, n_b, s // (nq * tb)),
        in_specs=[q_spec, kv_spec, kv_spec,
                  pl.BlockSpec((hp, 2, tb, BAND_ROWS), lambda h, b, i: (h, 0, 0, 0))],
        out_specs=q_spec,
        out_shape=jax.ShapeDtypeStruct((n_b, s, GROUP_WIDTH), BF16),
        scratch_shapes=[kv_scratch, kv_scratch],
        compiler_params=_params(3, VMEM_LIMIT_BYTES),
        name="band_prompt",
    )(q, k, v, bias)


def _fox_sample_kernel(q_ref, kc_ref, vc_ref, kn_ref, vn_ref, fc_ref, fn_ref, fq_ref, o_ref,
                       m_ref, l_ref, acc_ref, *, tc):
    c = pl.program_id(1)
    n_new = q_ref.shape[1]
    lanes = D_HEAD

    @pl.when(c == 0)
    def _():
        m_ref[...] = jnp.full(m_ref.shape, NEG, F32)
        l_ref[...] = jnp.zeros(l_ref.shape, F32)
        acc_ref[...] = jnp.zeros(acc_ref.shape, F32)

    def update(h, s, v):
        m_old = m_ref[h][:, :1]
        m_new = jnp.maximum(m_old, jnp.max(s, axis=1, keepdims=True))
        p = jnp.exp2(s - m_new)
        a = jnp.exp2(m_old - m_new)
        l_ref[h] = a * l_ref[h] + jnp.sum(p, axis=1, keepdims=True)
        m_ref[h] = jnp.broadcast_to(m_new, (n_new, lanes))
        acc_ref[h] = a * acc_ref[h] + _dot(p.astype(BF16), v)

    for h in range(N_HEADS):
        hl = slice(h * lanes, (h + 1) * lanes)
        rows = pl.ds(h, tc, stride=N_HEADS)
        s = (_dot_nt(q_ref[0, :, hl], kc_ref[0, rows, :].astype(BF16))
             + fq_ref[0, :, h:h + 1] - fc_ref[0, 0, h:h + 1, :])
        update(h, s, vc_ref[0, rows, :].astype(BF16))

    @pl.when(c == pl.num_programs(1) - 1)
    def _():
        for h in range(N_HEADS):
            hl = slice(h * lanes, (h + 1) * lanes)
            s = (_dot_nt(q_ref[0, :, hl], kn_ref[0, :, hl].astype(BF16))
                 + fq_ref[0, :, h:h + 1] - fn_ref[0, h:h + 1, :])
            row = lax.broadcasted_iota(jnp.int32, s.shape, 0)
            col = lax.broadcasted_iota(jnp.int32, s.shape, 1)
            update(h, jnp.where(col <= row, s, NEG), vn_ref[0, :, hl].astype(BF16))
            o_ref[0, :, hl] = (acc_ref[h] / l_ref[h]).astype(o_ref.dtype)


def _fox_sample(q, k_cache, v_cache, k_new, v_new, f_cache, f_new, f_q, *, tc=2048):
    n_b, n_new, _ = q.shape
    past = k_cache.shape[1] // N_HEADS
    n_chunks = past // tc
    f_cache = jnp.swapaxes(f_cache.reshape(n_b, N_HEADS, n_chunks, tc), 1, 2)
    new_spec = pl.BlockSpec((1, n_new, GROUP_WIDTH), lambda b, c: (b, 0, 0))
    cache_spec = pl.BlockSpec((1, tc * N_HEADS, D_HEAD), lambda b, c: (b, c, 0))
    state = pltpu.VMEM((N_HEADS, n_new, D_HEAD), F32)
    return pl.pallas_call(
        functools.partial(_fox_sample_kernel, tc=tc),
        grid=(n_b, n_chunks),
        in_specs=[new_spec, cache_spec, cache_spec, new_spec, new_spec,
                  pl.BlockSpec((1, 1, N_HEADS, tc), lambda b, c: (b, c, 0, 0)),
                  pl.BlockSpec((1, N_HEADS, n_new), lambda b, c: (b, 0, 0)),
                  pl.BlockSpec((1, n_new, N_HEADS), lambda b, c: (b, 0, 0))],
        out_specs=new_spec,
        out_shape=jax.ShapeDtypeStruct((n_b, n_new, GROUP_WIDTH), BF16),
        scratch_shapes=[state, state, state],
        compiler_params=_params(2, 40 * 1024 * 1024),
        name="fox_sample",
    )(q, k_cache, v_cache, k_new, v_new, f_cache, f_new, f_q)


def _band_sample_kernel(q_ref, kc_ref, vc_ref, kn_ref, vn_ref, bias_ref, o_ref):
    n_new = q_ref.shape[1]
    past = kc_ref.shape[1] // N_HEADS
    lanes = D_HEAD
    for h in range(N_HEADS):
        hl = slice(h * lanes, (h + 1) * lanes)
        rows = pl.ds(h, past, stride=N_HEADS)
        q = q_ref[0, :, hl]
        s_c = _dot_nt(q, kc_ref[0, rows, :].astype(BF16)) + bias_ref[h, 0]
        s_n = _dot_nt(q, kn_ref[0, :, hl].astype(BF16)) + bias_ref[h, 1][:, :n_new]
        m = jnp.maximum(jnp.max(s_c, axis=1, keepdims=True), jnp.max(s_n, axis=1, keepdims=True))
        p_c = jnp.exp2(s_c - m)
        p_n = jnp.exp2(s_n - m)
        l = jnp.sum(p_c, axis=1, keepdims=True) + jnp.sum(p_n, axis=1, keepdims=True)
        acc = (_dot(p_c.astype(BF16), vc_ref[0, rows, :].astype(BF16))
               + _dot(p_n.astype(BF16), vn_ref[0, :, hl].astype(BF16)))
        o_ref[0, :, hl] = (acc / l).astype(o_ref.dtype)


def _band_sample(q, k_cache, v_cache, k_new, v_new, bias):
    n_b, n_new, _ = q.shape
    new_spec = pl.BlockSpec((1, n_new, GROUP_WIDTH), lambda b: (b, 0, 0))
    cache_spec = pl.BlockSpec((1,) + k_cache.shape[1:], lambda b: (b, 0, 0))
    return pl.pallas_call(
        _band_sample_kernel,
        grid=(n_b,),
        in_specs=[new_spec, cache_spec, cache_spec, new_spec, new_spec,
                  pl.BlockSpec((N_HEADS, 2, n_new, BAND_ROWS), lambda b: (0, 0, 0, 0))],
        out_specs=new_spec,
        out_shape=jax.ShapeDtypeStruct((n_b, n_new, GROUP_WIDTH), BF16),
        compiler_params=_params(1, 40 * 1024 * 1024),
        name="band_sample",
    )(q, k_cache, v_cache, k_new, v_new, bias)


def _out_proj_kernel(fox_ref, band_ref, x_ref, ada_ref, w_ref, g_ref, b_ref, o_ref):
    bb, t, d = x_ref.shape
    gw = fox_ref.shape[2]
    for bs, rs, flat in _row_chunks(bb, t):
        x = x_ref[bs, rs, :]
        rows = flat.stop - flat.start
        y = (_dot(fox_ref[bs, rs, :].reshape(rows, gw), w_ref[:gw, :])
             + _dot(band_ref[bs, rs, :].reshape(rows, gw), w_ref[gw:, :]))
        z = ALPHA * x + ada_ref[bs, 2:3, :] * y.reshape(x.shape)
        o_ref[bs, rs, :] = _norm(z) * g_ref[...] + b_ref[...]


def _out_proj(fox, band, x, ada, w_out, g, b, *, bb, t):
    n_b, s, d = x.shape
    tok = lambda w: pl.BlockSpec((bb, t, w), lambda bi, i: (bi, i, 0))
    vec = pl.BlockSpec((1, d), lambda bi, i: (0, 0))
    return pl.pallas_call(
        _out_proj_kernel,
        grid=(n_b // bb, s // t),
        in_specs=[tok(GROUP_WIDTH), tok(GROUP_WIDTH), tok(d),
                  pl.BlockSpec((bb, 6, d), lambda bi, i: (bi, 0, 0)),
                  pl.BlockSpec(w_out.shape, lambda bi, i: (0, 0)), vec, vec],
        out_specs=tok(d),
        out_shape=jax.ShapeDtypeStruct(x.shape, F32),
        compiler_params=_params(2, VMEM_LIMIT_BYTES),
        name="out_proj",
    )(fox, band, x, ada, w_out, g.reshape(1, d), b.reshape(1, d))


def _mlp_kernel(x_ref, ada_ref, wu_ref, bu_ref, wd_ref, g_ref, b_ref, o_ref, u_ref):
    f = pl.program_id(2)
    last = pl.num_programs(2) - 1
    bb, t, d = x_ref.shape
    chunks = _row_chunks(bb, t)

    def hidden(u):
        hid = jnp.maximum(_dot(u, wu_ref[...]) + bu_ref[...], 0.0)
        return (hid * hid).astype(BF16)

    @pl.when(f == 0)
    def _():
        for bs, rs, flat in chunks:
            u = _norm(x_ref[bs, rs, :]) * (1.0 + ada_ref[bs, 4:5, :]) + ada_ref[bs, 3:4, :]
            u = u.reshape(flat.stop - flat.start, d).astype(BF16)
            u_ref[flat, :] = u
            o_ref[bs, rs, :] = _dot(hidden(u), wd_ref[...]).reshape(x_ref[bs, rs, :].shape)

    @pl.when(jnp.logical_and(f > 0, f < last))
    def _():
        o_ref[...] += _dot(hidden(u_ref[...]), wd_ref[...]).reshape(bb, t, d)

    @pl.when(f == last)
    def _():
        for bs, rs, flat in chunks:
            x = x_ref[bs, rs, :]
            y = o_ref[bs, rs, :] + _dot(hidden(u_ref[flat, :]), wd_ref[...]).reshape(x.shape)
            z = ALPHA * x + ada_ref[bs, 5:6, :] * y
            o_ref[bs, rs, :] = _norm(z) * g_ref[...] + b_ref[...]


def _mlp(x, ada, w_up, b_up, w_down, g, b, *, bb, t, tf=1024):
    n_b, s, d = x.shape
    d_ff = w_up.shape[1]
    assert d_ff // tf >= 2, "first and last hidden tiles are separate steps"
    tok = pl.BlockSpec((bb, t, d), lambda bi, i, f: (bi, i, 0))
    vec = pl.BlockSpec((1, d), lambda bi, i, f: (0, 0))
    return pl.pallas_call(
        _mlp_kernel,
        grid=(n_b // bb, s // t, d_ff // tf),
        in_specs=[tok, pl.BlockSpec((bb, 6, d), lambda bi, i, f: (bi, 0, 0)),
                  pl.BlockSpec((d, tf), lambda bi, i, f: (0, f)),
                  pl.BlockSpec((1, tf), lambda bi, i, f: (0, f)),
                  pl.BlockSpec((tf, d), lambda bi, i, f: (f, 0)), vec, vec],
        out_specs=tok,
        out_shape=jax.ShapeDtypeStruct(x.shape, F32),
        scratch_shapes=[pltpu.VMEM((bb * t, d), BF16)],
        compiler_params=_params(3, VMEM_LIMIT_BYTES),
        name="mlp",
    )(x, ada, w_up, b_up.reshape(1, d_ff), w_down, g.reshape(1, d), b.reshape(1, d))


def kernel(x_prompt, x_sample, cache_fox_k, cache_fox_v, cache_fox_logf, cache_band_k, cache_band_v,
           c_prompt, c_sample, w_ada, b_ada, w_in, b_forget, rel_bias, w_out, ln_mix_g, ln_mix_b,
           w_up, b_up, w_down, ln_mlp_g, ln_mlp_b):
    assert w_in.shape[0] == 1, "single-layer step"
    n_b, seq, d = x_prompt.shape
    n_bs, n_new, _ = x_sample.shape
    past = cache_fox_k.shape[2]
    gw = GROUP_WIDTH

    w = w_in[0]
    w_fox = w[:, :3 * gw].astype(BF16)
    w_band = w[:, 3 * gw + N_HEADS:].astype(BF16)
    wf_t = jnp.pad(w[:, 3 * gw:3 * gw + N_HEADS].T, ((0, 16 - N_HEADS), (0, 0))).astype(BF16)
    w_out_b = w_out[0].astype(BF16)
    w_up_b = w_up[0].astype(BF16)
    w_down_b = w_down[0].astype(BF16)

    ada = _ada(jnp.concatenate([c_prompt, c_sample], axis=0), w_ada[0], b_ada[0])
    ada = ada.reshape(n_b + n_bs, 6, d)
    ada_p, ada_s = ada[:n_b], ada[n_b:]
    bias = _band_bias(rel_bias[0], BAND_Q_BLOCK)

    qa, ka, va, qb, kb, vb, kb_tail, vb_tail, lft = _in_proj(
        x_prompt, ada_p, w_fox, w_band, wf_t, b_forget[0], bb=1, t=1024)
    f_p = _cumsum_lanes(lft)
    fox = _fox_prompt(qa, ka, va, f_p)
    band = _band_prompt(qb, kb, vb, bias)
    x1 = _out_proj(fox, band, x_prompt, ada_p, w_out_b, ln_mix_g[0], ln_mix_b[0], bb=1, t=1024)
    y_p = _mlp(x1, ada_p, w_up_b, b_up[0], w_down_b, ln_mlp_g[0], ln_mlp_b[0], bb=1, t=1024)

    heads = lambda a: a.reshape(1, a.shape[0], a.shape[1], N_HEADS, D_HEAD)
    p_states = (heads(ka), heads(va), jnp.swapaxes(lft, 1, 2)[None], heads(kb_tail), heads(vb_tail))

    qa_s, ka_s, va_s, qb_s, kb_s, vb_s, kb_tail_s, vb_tail_s, lft_s = _in_proj(
        x_sample, ada_s, w_fox, w_band, wf_t, b_forget[0], bb=n_bs, t=n_new)
    logf_s = jnp.transpose(lft_s.reshape(N_HEADS, n_bs, n_new), (1, 0, 2))
    total = past + n_new
    padded = -(-total // 128) * 128
    logf_all = jnp.concatenate(
        [jnp.swapaxes(cache_fox_logf[0], 1, 2), logf_s, jnp.zeros((n_bs, N_HEADS, padded - total), F32)], axis=2)
    f_s = _cumsum_lanes(logf_all)
    f_new = f_s[:, :, past:total]
    rows_view = lambda cache: cache.reshape(n_bs, cache.shape[2] * N_HEADS, D_HEAD)
    fox_s = _fox_sample(qa_s, rows_view(cache_fox_k), rows_view(cache_fox_v), ka_s, va_s,
                        f_s[:, :, :past], f_new, jnp.swapaxes(f_new, 1, 2))
    band_s = _band_sample(qb_s, rows_view(cache_band_k), rows_view(cache_band_v), kb_s, vb_s, bias)
    x1_s = _out_proj(fox_s, band_s, x_sample, ada_s, w_out_b, ln_mix_g[0], ln_mix_b[0], bb=n_bs, t=n_new)
    y_s = _mlp(x1_s, ada_s, w_up_b, b_up[0], w_down_b, ln_mlp_g[0], ln_mlp_b[0], bb=n_bs, t=n_new)
    s_states = (heads(ka_s), heads(va_s), jnp.swapaxes(logf_s, 1, 2)[None],
                heads(kb_tail_s), heads(vb_tail_s))

    return (y_p, y_s) + p_states + s_states
```

```python
import functools
import math

import jax
import jax.numpy as jnp
from jax import lax
from jax.experimental import pallas as pl
from jax.experimental.pallas import tpu as pltpu

D_HEAD = 128
N_HEADS = 8
GROUP_WIDTH = N_HEADS * D_HEAD
CHUNK = 64
BAND_ROWS = 512
BAND_Q_BLOCK = 256
MAX_REL = 128
N_REL = 2 * MAX_REL + 1
ALPHA = 2.0 ** 0.25
LN_EPS = 1e-5
NEG = -1e30
LOG2E = math.log2(math.e)
Q_SCALE = D_HEAD ** -0.5 * LOG2E

VMEM_LIMIT_BYTES = 60 * 1024 * 1024

F32 = jnp.float32
BF16 = jnp.bfloat16


def _params(n_axes, vmem=None):
    return pltpu.CompilerParams(dimension_semantics=("arbitrary",) * n_axes, vmem_limit_bytes=vmem)


def _norm(x):
    mu = jnp.mean(x, axis=-1, keepdims=True)
    xc = x - mu
    var = jnp.mean(xc * xc, axis=-1, keepdims=True)
    return xc * lax.rsqrt(var + LN_EPS)


def _dot_nt(a, b):
    return lax.dot_general(a, b, (((1,), (1,)), ((), ())), preferred_element_type=F32)


def _dot(a, b):
    return jnp.dot(a, b, preferred_element_type=F32)


ROW_CHUNK = 256


def _row_chunks(bb, t):
    if t >= ROW_CHUNK:
        return [(slice(b, b + 1), slice(r, r + ROW_CHUNK), slice(b * t + r, b * t + r + ROW_CHUNK))
                for b in range(bb) for r in range(0, t, ROW_CHUNK)]
    nb = min(ROW_CHUNK // t, bb)
    return [(slice(b, b + nb), slice(0, t), slice(b * t, (b + nb) * t)) for b in range(0, bb, nb)]


def _ada_kernel(c_ref, w_ref, b_ref, o_ref):
    a = jax.nn.silu(c_ref[...]).astype(BF16)
    o_ref[...] = _dot(a, w_ref[...].astype(BF16)) + b_ref[...]


def _ada(c, w_ada, b_ada, tn=1024):
    n_rows, d = c.shape
    n_out = w_ada.shape[1]
    return pl.pallas_call(
        _ada_kernel,
        grid=(n_out // tn,),
        in_specs=[pl.BlockSpec((n_rows, d), lambda j: (0, 0)),
                  pl.BlockSpec((d, tn), lambda j: (0, j)),
                  pl.BlockSpec((1, tn), lambda j: (0, j))],
        out_specs=pl.BlockSpec((n_rows, tn), lambda j: (0, j)),
        out_shape=jax.ShapeDtypeStruct((n_rows, n_out), F32),
        compiler_params=_params(1, 40 * 1024 * 1024),
        name="ada",
    )(c, w_ada, b_ada.reshape(1, n_out))


def _band_bias_kernel(tab_ref, o_ref):
    h = pl.program_id(0)
    t = BAND_ROWS
    w = 2 * t
    kidx = lax.broadcasted_iota(jnp.int32, (1, w), 1)
    d = jnp.where(kidx < t, kidx, kidx - w)
    idx_cur = jnp.clip(d, -MAX_REL, MAX_REL) + MAX_REL
    idx_prev = jnp.clip(d - t, -MAX_REL, MAX_REL) + MAX_REL

    def fill(r, rows):
        rc, rp = rows
        val = tab_ref[h, r] * LOG2E
        return jnp.where(idx_cur == r, val, rc), jnp.where(idx_prev == r, val, rp)

    rc, rp = lax.fori_loop(0, N_REL, fill, (jnp.zeros((1, w), F32), jnp.zeros((1, w), F32)))
    n_q = o_ref.shape[2]
    qrow = lax.broadcasted_iota(jnp.int32, (n_q, w), 0)

    def toeplitz(row):
        x = jnp.broadcast_to(row, (n_q, w))
        for bit in range(n_q.bit_length() - 1):
            x = jnp.where(((qrow >> bit) & 1) == 1, pltpu.roll(x, 1 << bit, 1), x)
        return x[:, :t]

    qc = lax.broadcasted_iota(jnp.int32, (n_q, t), 0) // CHUNK
    kc = lax.broadcasted_iota(jnp.int32, (n_q, t), 1) // CHUNK
    o_ref[0, 1] = jnp.where(kc <= qc, toeplitz(rc), NEG)
    o_ref[0, 0] = jnp.where(kc >= qc, toeplitz(rp), NEG)


def _band_bias(table, n_q):
    t = BAND_ROWS
    assert n_q & (n_q - 1) == 0 and n_q <= t
    return pl.pallas_call(
        _band_bias_kernel,
        grid=(N_HEADS,),
        in_specs=[pl.BlockSpec(memory_space=pltpu.SMEM)],
        out_specs=pl.BlockSpec((1, 2, n_q, t), lambda h: (h, 0, 0, 0)),
        out_shape=jax.ShapeDtypeStruct((N_HEADS, 2, n_q, t), F32),
        compiler_params=_params(1, 40 * 1024 * 1024),
        name="band_bias",
    )(table)


def _in_proj_kernel(x_ref, ada_ref, wfox_ref, wband_ref, wf_ref, bf_ref,
                    qa_ref, ka_ref, va_ref, qb_ref, kb_ref, vb_ref, kbt_ref, vbt_ref, lf_ref, u_ref,
                    *, tiles_per_out):
    j = pl.program_id(2)
    bb, t, d = x_ref.shape
    tn = wfox_ref.shape[1]
    tail = kbt_ref.shape[1]
    in_tail_block = pl.program_id(1) == pl.num_programs(1) - 1

    @pl.when(j == 0)
    def _():
        for bs, rs, flat in _row_chunks(bb, t):
            x = x_ref[bs, rs, :]
            u = _norm(x) * (1.0 + ada_ref[bs, 1:2, :]) + ada_ref[bs, 0:1, :]
            u = u.reshape(flat.stop - flat.start, d).astype(BF16)
            u_ref[flat, :] = u
            qa_ref[bs, rs, :] = (_dot(u, wfox_ref[...]) * Q_SCALE).astype(qa_ref.dtype).reshape(x.shape[:2] + (tn,))
        f = _dot_nt(wf_ref[...], u_ref[...])
        lf_ref[0] = jax.nn.log_sigmoid(f[:N_HEADS] + bf_ref[...])

    outs = ((qa_ref, Q_SCALE, None), (ka_ref, None, None), (va_ref, None, None),
            (qb_ref, Q_SCALE, None), (kb_ref, None, kbt_ref), (vb_ref, None, vbt_ref))
    for o, (ref, scale, tail_ref) in enumerate(outs):
        @pl.when(jnp.logical_and(j // tiles_per_out == o, j > 0))
        def _(ref=ref, scale=scale, tail_ref=tail_ref, w_ref=wfox_ref if o < 3 else wband_ref):
            acc = _dot(u_ref[...], w_ref[...])
            val = acc if scale is None else acc * scale
            ref[...] = val.astype(ref.dtype).reshape(bb, t, tn)
            if tail_ref is not None:
                @pl.when(in_tail_block)
                def _():
                    tail_ref[...] = acc.reshape(bb, t, tn)[:, t - tail:, :]


def _in_proj(x, ada, w_fox, w_band, wf_t, b_forget, *, bb, t, tn=512):
    n_b, s, d = x.shape
    m = bb * t
    tiles_per_out = GROUP_WIDTH // tn
    n_j = 6 * tiles_per_out

    n_i = s // t

    def out_spec(o):
        def index(b, i, j):
            return b, i, jnp.clip(j - o * tiles_per_out, 0, tiles_per_out - 1)
        return pl.BlockSpec((bb, t, tn), index)

    tail = min(BAND_ROWS, s)

    def tail_spec(o):
        def index(b, i, j):
            col = jnp.where(i == n_i - 1, jnp.clip(j - o * tiles_per_out, 0, tiles_per_out - 1), 0)
            return b, 0, col
        return pl.BlockSpec((bb, tail, tn), index)

    tok = lambda dt: jax.ShapeDtypeStruct((n_b, s, GROUP_WIDTH), dt)
    tail_shape = jax.ShapeDtypeStruct((n_b, tail, GROUP_WIDTH), F32)
    return pl.pallas_call(
        functools.partial(_in_proj_kernel, tiles_per_out=tiles_per_out),
        grid=(n_b // bb, s // t, n_j),
        in_specs=[pl.BlockSpec((bb, t, d), lambda b, i, j: (b, i, 0)),
                  pl.BlockSpec((bb, 6, d), lambda b, i, j: (b, 0, 0)),
                  pl.BlockSpec((d, tn), lambda b, i, j: (0, jnp.minimum(j, n_j // 2 - 1))),
                  pl.BlockSpec((d, tn), lambda b, i, j: (0, jnp.maximum(j - n_j // 2, 0))),
                  pl.BlockSpec(wf_t.shape, lambda b, i, j: (0, 0)),
                  pl.BlockSpec((N_HEADS, 1), lambda b, i, j: (0, 0))],
        out_specs=[out_spec(o) for o in range(6)] + [tail_spec(4), tail_spec(5)]
        + [pl.BlockSpec((1, N_HEADS, m), lambda b, i, j: (b, 0, i))],
        out_shape=[tok(BF16), tok(F32), tok(F32), tok(BF16), tok(BF16), tok(BF16), tail_shape, tail_shape,
                   jax.ShapeDtypeStruct((n_b // bb, N_HEADS, (s // t) * m), F32)],
        scratch_shapes=[pltpu.VMEM((m, d), BF16)],
        compiler_params=_params(3, VMEM_LIMIT_BYTES),
        name="in_proj",
    )(x, ada, w_fox, w_band, wf_t, b_forget.reshape(N_HEADS, 1))


def _cumsum_kernel(x_ref, o_ref):
    x = x_ref[0]
    n = x.shape[1]
    lane = lax.broadcasted_iota(jnp.int32, x.shape, 1)
    shift = 1
    while shift < n:
        x = x + jnp.where(lane >= shift, pltpu.roll(x, shift, 1), 0.0)
        shift *= 2
    o_ref[0] = x * LOG2E


def _cumsum_lanes(x):
    n_b, h, n = x.shape
    spec = pl.BlockSpec((1, h, n), lambda b: (b, 0, 0))
    return pl.pallas_call(
        _cumsum_kernel, grid=(n_b,), in_specs=[spec], out_specs=spec,
        out_shape=jax.ShapeDtypeStruct(x.shape, F32),
        compiler_params=_params(1), name="cumsum",
    )(x)


def _fox_prompt_kernel(q_ref, k_ref, v_ref, frow_ref, o_ref,
                       kb_ref, vb_ref, fq_ref, m_ref, l_ref, acc_ref, *, blk, strip, hp):
    qi = pl.program_id(2)
    lanes = D_HEAD
    n_col = blk // lanes
    head_lanes = [slice(hh * lanes, (hh + 1) * lanes) for hh in range(hp)]

    @pl.when(qi == 0)
    def _():
        for hh in range(hp):
            kb_ref[hh] = k_ref[0, :, head_lanes[hh]].astype(BF16)
            vb_ref[hh] = v_ref[0, :, head_lanes[hh]].astype(BF16)

    on_diag = (lax.broadcasted_iota(jnp.int32, (blk, blk), 0) == lax.broadcasted_iota(jnp.int32, (blk, blk), 1))
    for hh in range(hp):
        f_col = jnp.sum(jnp.where(on_diag, frow_ref[hh, qi], 0.0), axis=1, keepdims=True)
        fq_ref[hh] = jnp.broadcast_to(f_col, (blk, lanes))
    m_ref[...] = jnp.full(m_ref.shape, NEG, F32)
    l_ref[...] = jnp.zeros(l_ref.shape, F32)
    acc_ref[...] = jnp.zeros(acc_ref.shape, F32)

    def scores(hh, j):
        start = pl.multiple_of(j * blk, blk)
        return _dot_nt(q_ref[0, :, head_lanes[hh]], kb_ref[hh, pl.ds(start, blk), :])

    def softmax(hh, j, s_all, masked):
        f_keys = frow_ref[hh, j]
        p_strips = []
        for r in range(blk // strip):
            rows = slice(r * strip, (r + 1) * strip)
            fq = fq_ref[hh, rows]
            cols = []
            for cc in range(n_col):
                k0 = cc * lanes
                if masked and k0 >= (r + 1) * strip:
                    continue
                s = s_all[rows, k0:k0 + lanes] + fq - f_keys[:, k0:k0 + lanes]
                if masked and k0 + lanes - 1 > r * strip:
                    row = lax.broadcasted_iota(jnp.int32, s.shape, 0) + r * strip
                    col = lax.broadcasted_iota(jnp.int32, s.shape, 1) + k0
                    s = jnp.where(col <= row, s, NEG)
                cols.append(s)
            m_old = m_ref[hh, rows]
            m_blk = functools.reduce(jnp.maximum, cols)
            m_new = jnp.maximum(m_old, jnp.broadcast_to(jnp.max(m_blk, axis=1, keepdims=True), m_old.shape))
            ps = [jnp.exp2(s - m_new) for s in cols]
            a = jnp.exp2(m_old - m_new)
            l_ref[hh, rows] = a * l_ref[hh, rows] + functools.reduce(jnp.add, ps)
            m_ref[hh, rows] = m_new
            acc_ref[hh, rows] = a * acc_ref[hh, rows]
            ps = [p.astype(BF16) for p in ps] + [jnp.zeros((strip, lanes), BF16)] * (n_col - len(ps))
            p_strips.append(jnp.concatenate(ps, axis=1))
        return jnp.concatenate(p_strips, axis=0)

    def values(hh, j, p):
        start = pl.multiple_of(j * blk, blk)
        acc_ref[hh] += _dot(p, vb_ref[hh, pl.ds(start, blk), :])

    def block_steps(js, masked):
        chains = [(hh, j) for j in js for hh in range(hp)]
        s_next = scores(*chains[0])
        for n, (hh, j) in enumerate(chains):
            s_cur = s_next
            if n + 1 < len(chains):
                s_next = scores(*chains[n + 1])
            values(hh, j, softmax(hh, j, s_cur, masked))

    def pair_body(p, carry):
        block_steps([2 * p, 2 * p + 1], False)
        return carry

    lax.fori_loop(0, lax.shift_right_logical(qi, 1), pair_body, 0)

    @pl.when((qi & 1) == 1)
    def _():
        block_steps([qi - 1], False)

    block_steps([qi], True)
    for hh in range(hp):
        l = jnp.sum(l_ref[hh], axis=1, keepdims=True)
        o_ref[0, :, head_lanes[hh]] = (acc_ref[hh] / l).astype(o_ref.dtype)


def _fox_prompt(q, k, v, f, *, blk=512, strip=32, hp=4):
    n_b, s, _ = q.shape
    n_blk = s // blk
    n_hg = N_HEADS // hp
    frow = f.reshape(n_b * N_HEADS, n_blk, 1, blk)
    kv_spec = pl.BlockSpec((1, s, hp * D_HEAD), lambda b, h, i: (b, 0, h))
    q_spec = pl.BlockSpec((1, blk, hp * D_HEAD), lambda b, h, i: (b, i, h))
    state = pltpu.VMEM((hp, blk, D_HEAD), F32)
    return pl.pallas_call(
        functools.partial(_fox_prompt_kernel, blk=blk, strip=strip, hp=hp),
        grid=(n_b, n_hg, n_blk),
        in_specs=[q_spec, kv_spec, kv_spec,
                  pl.BlockSpec((hp, n_blk, 1, blk), lambda b, h, i: (b * n_hg + h, 0, 0, 0))],
        out_specs=q_spec,
        out_shape=jax.ShapeDtypeStruct((n_b, s, GROUP_WIDTH), BF16),
        scratch_shapes=[pltpu.VMEM((hp, s, D_HEAD), BF16), pltpu.VMEM((hp, s, D_HEAD), BF16),
                        state, state, state, state],
        compiler_params=_params(3, VMEM_LIMIT_BYTES),
        name="fox_prompt",
    )(q, k, v, frow)


def _band_prompt_kernel(q_ref, k_ref, v_ref, bias_ref, o_ref, kb_ref, vb_ref, *, tb, nq, strip, hp):
    i = pl.program_id(2)
    past = BAND_ROWS
    lanes = D_HEAD
    win = past + tb
    head_lanes = [slice(hh * lanes, (hh + 1) * lanes) for hh in range(hp)]

    @pl.when(i == 0)
    def _():
        for hh in range(hp):
            kb_ref[hh, :past, :] = jnp.zeros((past, D_HEAD), BF16)
            vb_ref[hh, :past, :] = jnp.zeros((past, D_HEAD), BF16)
            kb_ref[hh, past:, :] = k_ref[0, :, head_lanes[hh]]
            vb_ref[hh, past:, :] = v_ref[0, :, head_lanes[hh]]

    chains = [(hh, qq) for qq in range(nq) for hh in range(hp)]

    def block_start(qq):
        return pl.multiple_of((i * nq + qq) * tb, tb)

    def scores(chain):
        hh, qq = chain
        q = q_ref[0, qq * tb:(qq + 1) * tb, head_lanes[hh]]
        return _dot_nt(q, kb_ref[hh, pl.ds(block_start(qq), win), :])

    def softmax(chain, s_all):
        hh, qq = chain
        first_valid = past - block_start(qq)
        p_strips, l_strips = [], []
        for r in range(tb // strip):
            rows = slice(r * strip, (r + 1) * strip)
            q_chunk = (r * strip) // CHUNK
            cols = {}
            for cc in range(win // lanes):
                k0 = cc * lanes
                if k0 < past:
                    if (k0 + lanes - 1) // CHUNK < q_chunk:
                        continue
                    s = s_all[rows, k0:k0 + lanes] + bias_ref[hh, 0, rows, k0:k0 + lanes]
                    col = lax.broadcasted_iota(jnp.int32, s.shape, 1) + k0
                    s = jnp.where(col >= first_valid, s, NEG)
                else:
                    if (k0 - past) // CHUNK > q_chunk:
                        continue
                    s = s_all[rows, k0:k0 + lanes] + bias_ref[hh, 1, rows, k0 - past:k0 - past + lanes]
                cols[cc] = s
            m_blk = functools.reduce(jnp.maximum, cols.values())
            m = jnp.broadcast_to(jnp.max(m_blk, axis=1, keepdims=True), m_blk.shape)
            ps = {cc: jnp.exp2(s - m) for cc, s in cols.items()}
            l_strips.append(functools.reduce(jnp.add, ps.values()))
            zero = jnp.zeros((strip, lanes), BF16)
            p_strips.append(jnp.concatenate(
                [ps[cc].astype(BF16) if cc in ps else zero for cc in range(win // lanes)], axis=1))
        return jnp.concatenate(p_strips, axis=0), jnp.concatenate(l_strips, axis=0)

    def values(chain, p, l_part):
        hh, qq = chain
        acc = _dot(p, vb_ref[hh, pl.ds(block_start(qq), win), :])
        l = jnp.sum(l_part, axis=1, keepdims=True)
        o_ref[0, qq * tb:(qq + 1) * tb, head_lanes[hh]] = (acc / l).astype(o_ref.dtype)

    s_next = scores(chains[0])
    for n, chain in enumerate(chains):
        s_cur = s_next
        if n + 1 < len(chains):
            s_next = scores(chains[n + 1])
        values(chain, *softmax(chain, s_cur))


def _band_prompt(q, k, v, bias, *, tb=BAND_Q_BLOCK, nq=2, strip=32, hp=4):
    n_b, s, _ = q.shape
    kv_spec = pl.BlockSpec((1, s, hp * D_HEAD), lambda h, b, i: (b, 0, h))
    q_spec = pl.BlockSpec((1, nq * tb, hp * D_HEAD), lambda h, b, i: (b, i, h))
    kv_scratch = pltpu.VMEM((hp, BAND_ROWS + s, D_HEAD), BF16)
    return pl.pallas_call(
        functools.partial(_band_prompt_kernel, tb=tb, nq=nq, strip=strip, hp=hp),
        grid=(N_HEADS // hp, n_b, s // (nq * tb)),
        in_specs=[q_spec, kv_spec, kv_spec,
                  pl.BlockSpec((hp, 2, tb, BAND_ROWS), lambda h, b, i: (h, 0, 0, 0))],
        out_specs=q_spec,
        out_shape=jax.ShapeDtypeStruct((n_b, s, GROUP_WIDTH), BF16),
        scratch_shapes=[kv_scratch, kv_scratch],
        compiler_params=_params(3, VMEM_LIMIT_BYTES),
        name="band_prompt",
    )(q, k, v, bias)


def _fox_sample_kernel(q_ref, kc_ref, vc_ref, kn_ref, vn_ref, fc_ref, fn_ref, fq_ref, o_ref,
                       m_ref, l_ref, acc_ref, *, tc):
    c = pl.program_id(1)
    n_new = q_ref.shape[1]
    lanes = D_HEAD

    @pl.when(c == 0)
    def _():
        m_ref[...] = jnp.full(m_ref.shape, NEG, F32)
        l_ref[...] = jnp.zeros(l_ref.shape, F32)
        acc_ref[...] = jnp.zeros(acc_ref.shape, F32)

    def update(h, s, v):
        m_old = m_ref[h][:, :1]
        m_new = jnp.maximum(m_old, jnp.max(s, axis=1, keepdims=True))
        p = jnp.exp2(s - m_new)
        a = jnp.exp2(m_old - m_new)
        l_ref[h] = a * l_ref[h] + jnp.sum(p, axis=1, keepdims=True)
        m_ref[h] = jnp.broadcast_to(m_new, (n_new, lanes))
        acc_ref[h] = a * acc_ref[h] + _dot(p.astype(BF16), v)

    for h in range(N_HEADS):
        hl = slice(h * lanes, (h + 1) * lanes)
        rows = pl.ds(h, tc, stride=N_HEADS)
        s = (_dot_nt(q_ref[0, :, hl], kc_ref[0, rows, :].astype(BF16))
             + fq_ref[0, :, h:h + 1] - fc_ref[0, 0, h:h + 1, :])
        update(h, s, vc_ref[0, rows, :].astype(BF16))

    @pl.when(c == pl.num_programs(1) - 1)
    def _():
        for h in range(N_HEADS):
            hl = slice(h * lanes, (h + 1) * lanes)
            s = (_dot_nt(q_ref[0, :, hl], kn_ref[0, :, hl].astype(BF16))
                 + fq_ref[0, :, h:h + 1] - fn_ref[0, h:h + 1, :])
            row = lax.broadcasted_iota(jnp.int32, s.shape, 0)
            col = lax.broadcasted_iota(jnp.int32, s.shape, 1)
            update(h, jnp.where(col <= row, s, NEG), vn_ref[0, :, hl].astype(BF16))
            o_ref[0, :, hl] = (acc_ref[h] / l_ref[h]).astype(o_ref.dtype)


def _fox_sample(q, k_cache, v_cache, k_new, v_new, f_cache, f_new, f_q, *, tc=2048):
    n_b, n_new, _ = q.shape
    past = k_cache.shape[1] // N_HEADS
    n_chunks = past // tc
    f_cache = jnp.swapaxes(f_cache.reshape(n_b, N_HEADS, n_chunks, tc), 1, 2)
    new_spec = pl.BlockSpec((1, n_new, GROUP_WIDTH), lambda b, c: (b, 0, 0))
    cache_spec = pl.BlockSpec((1, tc * N_HEADS, D_HEAD), lambda b, c: (b, c, 0))
    state = pltpu.VMEM((N_HEADS, n_new, D_HEAD), F32)
    return pl.pallas_call(
        functools.partial(_fox_sample_kernel, tc=tc),
        grid=(n_b, n_chunks),
        in_specs=[new_spec, cache_spec, cache_spec, new_spec, new_spec,
                  pl.BlockSpec((1, 1, N_HEADS, tc), lambda b, c: (b, c, 0, 0)),
                  pl.BlockSpec((1, N_HEADS, n_new), lambda b, c: (b, 0, 0)),
                  pl.BlockSpec((1, n_new, N_HEADS), lambda b, c: (b, 0, 0))],
        out_specs=new_spec,
        out_shape=jax.ShapeDtypeStruct((n_b, n_new, GROUP_WIDTH), BF16),
        scratch_shapes=[state, state, state],
        compiler_params=_params(2, 40 * 1024 * 1024),
        name="fox_sample",
    )(q, k_cache, v_cache, k_new, v_new, f_cache, f_new, f_q)


def _band_sample_kernel(q_ref, kc_ref, vc_ref, kn_ref, vn_ref, bias_ref, o_ref):
    n_new = q_ref.shape[1]
    past = kc_ref.shape[1] // N_HEADS
    lanes = D_HEAD
    for h in range(N_HEADS):
        hl = slice(h * lanes, (h + 1) * lanes)
        rows = pl.ds(h, past, stride=N_HEADS)
        q = q_ref[0, :, hl]
        s_c = _dot_nt(q, kc_ref[0, rows, :].astype(BF16)) + bias_ref[h, 0]
        s_n = _dot_nt(q, kn_ref[0, :, hl].astype(BF16)) + bias_ref[h, 1][:, :n_new]
        m = jnp.maximum(jnp.max(s_c, axis=1, keepdims=True), jnp.max(s_n, axis=1, keepdims=True))
        p_c = jnp.exp2(s_c - m)
        p_n = jnp.exp2(s_n - m)
        l = jnp.sum(p_c, axis=1, keepdims=True) + jnp.sum(p_n, axis=1, keepdims=True)
        acc = (_dot(p_c.astype(BF16), vc_ref[0, rows, :].astype(BF16))
               + _dot(p_n.astype(BF16), vn_ref[0, :, hl].astype(BF16)))
        o_ref[0, :, hl] = (acc / l).astype(o_ref.dtype)


def _band_sample(q, k_cache, v_cache, k_new, v_new, bias):
    n_b, n_new, _ = q.shape
    new_spec = pl.BlockSpec((1, n_new, GROUP_WIDTH), lambda b: (b, 0, 0))
    cache_spec = pl.BlockSpec((1,) + k_cache.shape[1:], lambda b: (b, 0, 0))
    return pl.pallas_call(
        _band_sample_kernel,
        grid=(n_b,),
        in_specs=[new_spec, cache_spec, cache_spec, new_spec, new_spec,
                  pl.BlockSpec((N_HEADS, 2, n_new, BAND_ROWS), lambda b: (0, 0, 0, 0))],
        out_specs=new_spec,
        out_shape=jax.ShapeDtypeStruct((n_b, n_new, GROUP_WIDTH), BF16),
        compiler_params=_params(1, 40 * 1024 * 1024),
        name="band_sample",
    )(q, k_cache, v_cache, k_new, v_new, bias)


def _out_proj_kernel(fox_ref, band_ref, x_ref, ada_ref, w_ref, g_ref, b_ref, o_ref):
    bb, t, d = x_ref.shape
    gw = fox_ref.shape[2]
    for bs, rs, flat in _row_chunks(bb, t):
        x = x_ref[bs, rs, :]
        rows = flat.stop - flat.start
        y = (_dot(fox_ref[bs, rs, :].reshape(rows, gw), w_ref[:gw, :])
             + _dot(band_ref[bs, rs, :].reshape(rows, gw), w_ref[gw:, :]))
        z = ALPHA * x + ada_ref[bs, 2:3, :] * y.reshape(x.shape)
        o_ref[bs, rs, :] = _norm(z) * g_ref[...] + b_ref[...]


def _out_proj(fox, band, x, ada, w_out, g, b, *, bb, t):
    n_b, s, d = x.shape
    tok = lambda w: pl.BlockSpec((bb, t, w), lambda bi, i: (bi, i, 0))
    vec = pl.BlockSpec((1, d), lambda bi, i: (0, 0))
    return pl.pallas_call(
        _out_proj_kernel,
        grid=(n_b // bb, s // t),
        in_specs=[tok(GROUP_WIDTH), tok(GROUP_WIDTH), tok(d),
                  pl.BlockSpec((bb, 6, d), lambda bi, i: (bi, 0, 0)),
                  pl.BlockSpec(w_out.shape, lambda bi, i: (0, 0)), vec, vec],
        out_specs=tok(d),
        out_shape=jax.ShapeDtypeStruct(x.shape, F32),
        compiler_params=_params(2, VMEM_LIMIT_BYTES),
        name="out_proj",
    )(fox, band, x, ada, w_out, g.reshape(1, d), b.reshape(1, d))


def _mlp_kernel(x_ref, ada_ref, wu_ref, bu_ref, wd_ref, g_ref, b_ref, o_ref, u_ref):
    f = pl.program_id(2)
    last = pl.num_programs(2) - 1
    bb, t, d = x_ref.shape
    chunks = _row_chunks(bb, t)

    def hidden(u):
        hid = jnp.maximum(_dot(u, wu_ref[...]) + bu_ref[...], 0.0)
        return (hid * hid).astype(BF16)

    @pl.when(f == 0)
    def _():
        for bs, rs, flat in chunks:
            u = _norm(x_ref[bs, rs, :]) * (1.0 + ada_ref[bs, 4:5, :]) + ada_ref[bs, 3:4, :]
            u = u.reshape(flat.stop - flat.start, d).astype(BF16)
            u_ref[flat, :] = u
            o_ref[bs, rs, :] = _dot(hidden(u), wd_ref[...]).reshape(x_ref[bs, rs, :].shape)

    @pl.when(jnp.logical_and(f > 0, f < last))
    def _():
        o_ref[...] += _dot(hidden(u_ref[...]), wd_ref[...]).reshape(bb, t, d)

    @pl.when(f == last)
    def _():
        for bs, rs, flat in chunks:
            x = x_ref[bs, rs, :]
            y = o_ref[bs, rs, :] + _dot(hidden(u_ref[flat, :]), wd_ref[...]).reshape(x.shape)
            z = ALPHA * x + ada_ref[bs, 5:6, :] * y
            o_ref[bs, rs, :] = _norm(z) * g_ref[...] + b_ref[...]


def _mlp(x, ada, w_up, b_up, w_down, g, b, *, bb, t, tf=1024):
    n_b, s, d = x.shape
    d_ff = w_up.shape[1]
    assert d_ff // tf >= 2, "first and last hidden tiles are separate steps"
    tok = pl.BlockSpec((bb, t, d), lambda bi, i, f: (bi, i, 0))
    vec = pl.BlockSpec((1, d), lambda bi, i, f: (0, 0))
    return pl.pallas_call(
        _mlp_kernel,
        grid=(n_b // bb, s // t, d_ff // tf),
        in_specs=[tok, pl.BlockSpec((bb, 6, d), lambda bi, i, f: (bi, 0, 0)),
                  pl.BlockSpec((d, tf), lambda bi, i, f: (0, f)),
                  pl.BlockSpec((1, tf), lambda bi, i, f: (0, f)),
                  pl.BlockSpec((tf, d), lambda bi, i, f: (f, 0)), vec, vec],
        out_specs=tok,
        out_shape=jax.ShapeDtypeStruct(x.shape, F32),
        scratch_shapes=[pltpu.VMEM((bb * t, d), BF16)],
        compiler_params=_params(3, VMEM_LIMIT_BYTES),
        name="mlp",
    )(x, ada, w_up, b_up.reshape(1, d_ff), w_down, g.reshape(1, d), b.reshape(1, d))


def kernel(x_prompt, x_sample, cache_fox_k, cache_fox_v, cache_fox_logf, cache_band_k, cache_band_v,
           c_prompt, c_sample, w_ada, b_ada, w_in, b_forget, rel_bias, w_out, ln_mix_g, ln_mix_b,
           w_up, b_up, w_down, ln_mlp_g, ln_mlp_b):
    assert w_in.shape[0] == 1, "single-layer step"
    n_b, seq, d = x_prompt.shape
    n_bs, n_new, _ = x_sample.shape
    past = cache_fox_k.shape[2]
    gw = GROUP_WIDTH

    w = w_in[0]
    w_fox = w[:, :3 * gw].astype(BF16)
    w_band = w[:, 3 * gw + N_HEADS:].astype(BF16)
    wf_t = jnp.pad(w[:, 3 * gw:3 * gw + N_HEADS].T, ((0, 16 - N_HEADS), (0, 0))).astype(BF16)
    w_out_b = w_out[0].astype(BF16)
    w_up_b = w_up[0].astype(BF16)
    w_down_b = w_down[0].astype(BF16)

    ada = _ada(jnp.concatenate([c_prompt, c_sample], axis=0), w_ada[0], b_ada[0])
    ada = ada.reshape(n_b + n_bs, 6, d)
    ada_p, ada_s = ada[:n_b], ada[n_b:]
    bias = _band_bias(rel_bias[0], BAND_Q_BLOCK)

    qa, ka, va, qb, kb, vb, kb_tail, vb_tail, lft = _in_proj(
        x_prompt, ada_p, w_fox, w_band, wf_t, b_forget[0], bb=1, t=1024)
    f_p = _cumsum_lanes(lft)
    fox = _fox_prompt(qa, ka, va, f_p)
    band = _band_prompt(qb, kb, vb, bias)
    x1 = _out_proj(fox, band, x_prompt, ada_p, w_out_b, ln_mix_g[0], ln_mix_b[0], bb=1, t=1024)
    y_p = _mlp(x1, ada_p, w_up_b, b_up[0], w_down_b, ln_mlp_g[0], ln_mlp_b[0], bb=1, t=1024)

    heads = lambda a: a.reshape(1, a.shape[0], a.shape[1], N_HEADS, D_HEAD)
    p_states = (heads(ka), heads(va), jnp.swapaxes(lft, 1, 2)[None], heads(kb_tail), heads(vb_tail))

    qa_s, ka_s, va_s, qb_s, kb_s, vb_s, kb_tail_s, vb_tail_s, lft_s = _in_proj(
        x_sample, ada_s, w_fox, w_band, wf_t, b_forget[0], bb=n_bs, t=n_new)
    logf_s = jnp.transpose(lft_s.reshape(N_HEADS, n_bs, n_new), (1, 0, 2))
    total = past + n_new
    padded = -(-total // 128) * 128
    logf_all = jnp.concatenate(
        [jnp.swapaxes(cache_fox_logf[0], 1, 2), logf_s, jnp.zeros((n_bs, N_HEADS, padded - total), F32)], axis=2)
    f_s = _cumsum_lanes(logf_all)
    f_new = f_s[:, :, past:total]
    rows_view = lambda cache: cache.reshape(n_bs, cache.shape[2] * N_HEADS, D_HEAD)
    fox_s = _fox_sample(qa_s, rows_view(cache_fox_k), rows_view(cache_fox_v), ka_s, va_s,
                        f_s[:, :, :past], f_new, jnp.swapaxes(f_new, 1, 2))
    band_s = _band_sample(qb_s, rows_view(cache_band_k), rows_view(cache_band_v), kb_s, vb_s, bias)
    x1_s = _out_proj(fox_s, band_s, x_sample, ada_s, w_out_b, ln_mix_g[0], ln_mix_b[0], bb=n_bs, t=n_new)
    y_s = _mlp(x1_s, ada_s, w_up_b, b_up[0], w_down_b, ln_mlp_g[0], ln_mlp_b[0], bb=n_bs, t=n_new)
    s_states = (heads(ka_s), heads(va_s), jnp.swapaxes(logf_s, 1, 2)[None],
                heads(kb_tail_s), heads(vb_tail_s))

    return (y_p, y_s) + p_states + s_states
```

```python
import functools
import math

import jax
import jax.numpy as jnp
from jax import lax
from jax.experimental import pallas as pl
from jax.experimental.pallas import tpu as pltpu

D_HEAD = 128
N_HEADS = 8
GROUP_WIDTH = N_HEADS * D_HEAD
CHUNK = 64
BAND_ROWS = 512
BAND_Q_BLOCK = 256
MAX_REL = 128
N_REL = 2 * MAX_REL + 1
ALPHA = 2.0 ** 0.25
LN_EPS = 1e-5
NEG = -1e30
LOG2E = math.log2(math.e)
Q_SCALE = D_HEAD ** -0.5 * LOG2E

VMEM_LIMIT_BYTES = 60 * 1024 * 1024

F32 = jnp.float32
BF16 = jnp.bfloat16


def _params(n_axes, vmem=None):
    return pltpu.CompilerParams(dimension_semantics=("arbitrary",) * n_axes, vmem_limit_bytes=vmem)


def _norm(x):
    mu = jnp.mean(x, axis=-1, keepdims=True)
    xc = x - mu
    var = jnp.mean(xc * xc, axis=-1, keepdims=True)
    return xc * lax.rsqrt(var + LN_EPS)


def _dot_nt(a, b):
    return lax.dot_general(a, b, (((1,), (1,)), ((), ())), preferred_element_type=F32)


def _dot(a, b):
    return jnp.dot(a, b, preferred_element_type=F32)


ROW_CHUNK = 256


def _row_chunks(bb, t):
    if t >= ROW_CHUNK:
        return [(slice(b, b + 1), slice(r, r + ROW_CHUNK), slice(b * t + r, b * t + r + ROW_CHUNK))
                for b in range(bb) for r in range(0, t, ROW_CHUNK)]
    nb = min(ROW_CHUNK // t, bb)
    return [(slice(b, b + nb), slice(0, t), slice(b * t, (b + nb) * t)) for b in range(0, bb, nb)]


def _ada_kernel(c_ref, w_ref, b_ref, o_ref):
    a = jax.nn.silu(c_ref[...]).astype(BF16)
    o_ref[...] = _dot(a, w_ref[...].astype(BF16)) + b_ref[...]


def _ada(c, w_ada, b_ada, tn=1024):
    n_rows, d = c.shape
    n_out = w_ada.shape[1]
    return pl.pallas_call(
        _ada_kernel,
        grid=(n_out // tn,),
        in_specs=[pl.BlockSpec((n_rows, d), lambda j: (0, 0)),
                  pl.BlockSpec((d, tn), lambda j: (0, j)),
                  pl.BlockSpec((1, tn), lambda j: (0, j))],
        out_specs=pl.BlockSpec((n_rows, tn), lambda j: (0, j)),
        out_shape=jax.ShapeDtypeStruct((n_rows, n_out), F32),
        compiler_params=_params(1, 40 * 1024 * 1024),
        name="ada",
    )(c, w_ada, b_ada.reshape(1, n_out))


def _band_bias_kernel(tab_ref, o_ref):
    h = pl.program_id(0)
    t = BAND_ROWS
    w = 2 * t
    kidx = lax.broadcasted_iota(jnp.int32, (1, w), 1)
    d = jnp.where(kidx < t, kidx, kidx - w)
    idx_cur = jnp.clip(d, -MAX_REL, MAX_REL) + MAX_REL
    idx_prev = jnp.clip(d - t, -MAX_REL, MAX_REL) + MAX_REL

    def fill(r, rows):
        rc, rp = rows
        val = tab_ref[h, r] * LOG2E
        return jnp.where(idx_cur == r, val, rc), jnp.where(idx_prev == r, val, rp)

    rc, rp = lax.fori_loop(0, N_REL, fill, (jnp.zeros((1, w), F32), jnp.zeros((1, w), F32)))
    n_q = o_ref.shape[2]
    qrow = lax.broadcasted_iota(jnp.int32, (n_q, w), 0)

    def toeplitz(row):
        x = jnp.broadcast_to(row, (n_q, w))
        for bit in range(n_q.bit_length() - 1):
            x = jnp.where(((qrow >> bit) & 1) == 1, pltpu.roll(x, 1 << bit, 1), x)
        return x[:, :t]

    qc = lax.broadcasted_iota(jnp.int32, (n_q, t), 0) // CHUNK
    kc = lax.broadcasted_iota(jnp.int32, (n_q, t), 1) // CHUNK
    o_ref[0, 1] = jnp.where(kc <= qc, toeplitz(rc), NEG)
    o_ref[0, 0] = jnp.where(kc >= qc, toeplitz(rp), NEG)


def _band_bias(table, n_q):
    t = BAND_ROWS
    assert n_q & (n_q - 1) == 0 and n_q <= t
    return pl.pallas_call(
        _band_bias_kernel,
        grid=(N_HEADS,),
        in_specs=[pl.BlockSpec(memory_space=pltpu.SMEM)],
        out_specs=pl.BlockSpec((1, 2, n_q, t), lambda h: (h, 0, 0, 0)),
        out_shape=jax.ShapeDtypeStruct((N_HEADS, 2, n_q, t), F32),
        compiler_params=_params(1, 40 * 1024 * 1024),
        name="band_bias",
    )(table)


def _in_proj_kernel(x_ref, ada_ref, wfox_ref, wband_ref, wf_ref, bf_ref,
                    qa_ref, ka_ref, va_ref, qb_ref, kb_ref, vb_ref, kbt_ref, vbt_ref, lf_ref, u_ref,
                    *, tiles_per_out):
    j = pl.program_id(2)
    bb, t, d = x_ref.shape
    tn = wfox_ref.shape[1]
    tail = kbt_ref.shape[1]
    in_tail_block = pl.program_id(1) == pl.num_programs(1) - 1

    @pl.when(j == 0)
    def _():
        for bs, rs, flat in _row_chunks(bb, t):
            x = x_ref[bs, rs, :]
            u = _norm(x) * (1.0 + ada_ref[bs, 1:2, :]) + ada_ref[bs, 0:1, :]
            u = u.reshape(flat.stop - flat.start, d).astype(BF16)
            u_ref[flat, :] = u
            qa_ref[bs, rs, :] = (_dot(u, wfox_ref[...]) * Q_SCALE).astype(qa_ref.dtype).reshape(x.shape[:2] + (tn,))
        f = _dot_nt(wf_ref[...], u_ref[...])
        lf_ref[0] = jax.nn.log_sigmoid(f[:N_HEADS] + bf_ref[...])

    outs = ((qa_ref, Q_SCALE, None), (ka_ref, None, None), (va_ref, None, None),
            (qb_ref, Q_SCALE, None), (kb_ref, None, kbt_ref), (vb_ref, None, vbt_ref))
    for o, (ref, scale, tail_ref) in enumerate(outs):
        @pl.when(jnp.logical_and(j // tiles_per_out == o, j > 0))
        def _(ref=ref, scale=scale, tail_ref=tail_ref, w_ref=wfox_ref if o < 3 else wband_ref):
            acc = _dot(u_ref[...], w_ref[...])
            val = acc if scale is None else acc * scale
            ref[...] = val.astype(ref.dtype).reshape(bb, t, tn)
            if tail_ref is not None:
                @pl.when(in_tail_block)
                def _():
                    tail_ref[...] = acc.reshape(bb, t, tn)[:, t - tail:, :]


def _in_proj(x, ada, w_fox, w_band, wf_t, b_forget, *, bb, t, tn=512):
    n_b, s, d = x.shape
    m = bb * t
    tiles_per_out = GROUP_WIDTH // tn
    n_j = 6 * tiles_per_out

    n_i = s // t

    def out_spec(o):
        def index(b, i, j):
            return b, i, jnp.clip(j - o * tiles_per_out, 0, tiles_per_out - 1)
        return pl.BlockSpec((bb, t, tn), index)

    tail = min(BAND_ROWS, s)

    def tail_spec(o):
        def index(b, i, j):
            col = jnp.where(i == n_i - 1, jnp.clip(j - o * tiles_per_out, 0, tiles_per_out - 1), 0)
            return b, 0, col
        return pl.BlockSpec((bb, tail, tn), index)

    tok = lambda dt: jax.ShapeDtypeStruct((n_b, s, GROUP_WIDTH), dt)
    tail_shape = jax.ShapeDtypeStruct((n_b, tail, GROUP_WIDTH), F32)
    return pl.pallas_call(
        functools.partial(_in_proj_kernel, tiles_per_out=tiles_per_out),
        grid=(n_b // bb, s // t, n_j),
        in_specs=[pl.BlockSpec((bb, t, d), lambda b, i, j: (b, i, 0)),
                  pl.BlockSpec((bb, 6, d), lambda b, i, j: (b, 0, 0)),
                  pl.BlockSpec((d, tn), lambda b, i, j: (0, jnp.minimum(j, n_j // 2 - 1))),
                  pl.BlockSpec((d, tn), lambda b, i, j: (0, jnp.maximum(j - n_j // 2, 0))),
                  pl.BlockSpec(wf_t.shape, lambda b, i, j: (0, 0)),
                  pl.BlockSpec((N_HEADS, 1), lambda b, i, j: (0, 0))],
        out_specs=[out_spec(o) for o in range(6)] + [tail_spec(4), tail_spec(5)]
        + [pl.BlockSpec((1, N_HEADS, m), lambda b, i, j: (b, 0, i))],
        out_shape=[tok(BF16), tok(F32), tok(F32), tok(BF16), tok(BF16), tok(BF16), tail_shape, tail_shape,
                   jax.ShapeDtypeStruct((n_b // bb, N_HEADS, (s // t) * m), F32)],
        scratch_shapes=[pltpu.VMEM((m, d), BF16)],
        compiler_params=_params(3, VMEM_LIMIT_BYTES),
        name="in_proj",
    )(x, ada, w_fox, w_band, wf_t, b_forget.reshape(N_HEADS, 1))


def _cumsum_kernel(x_ref, o_ref):
    x = x_ref[0]
    n = x.shape[1]
    lane = lax.broadcasted_iota(jnp.int32, x.shape, 1)
    shift = 1
    while shift < n:
        x = x + jnp.where(lane >= shift, pltpu.roll(x, shift, 1), 0.0)
        shift *= 2
    o_ref[0] = x * LOG2E


def _cumsum_lanes(x):
    n_b, h, n = x.shape
    spec = pl.BlockSpec((1, h, n), lambda b: (b, 0, 0))
    return pl.pallas_call(
        _cumsum_kernel, grid=(n_b,), in_specs=[spec], out_specs=spec,
        out_shape=jax.ShapeDtypeStruct(x.shape, F32),
        compiler_params=_params(1), name="cumsum",
    )(x)


def _fox_prompt_kernel(q_ref, k_ref, v_ref, frow_ref, o_ref,
                       kb_ref, vb_ref, fq_ref, m_ref, l_ref, acc_ref, *, blk, strip, hp):
    qi = pl.program_id(2)
    lanes = D_HEAD
    n_col = blk // lanes
    head_lanes = [slice(hh * lanes, (hh + 1) * lanes) for hh in range(hp)]

    @pl.when(qi == 0)
    def _():
        for hh in range(hp):
            kb_ref[hh] = k_ref[0, :, head_lanes[hh]].astype(BF16)
            vb_ref[hh] = v_ref[0, :, head_lanes[hh]].astype(BF16)

    on_diag = (lax.broadcasted_iota(jnp.int32, (blk, blk), 0) == lax.broadcasted_iota(jnp.int32, (blk, blk), 1))
    for hh in range(hp):
        f_col = jnp.sum(jnp.where(on_diag, frow_ref[hh, qi], 0.0), axis=1, keepdims=True)
        fq_ref[hh] = jnp.broadcast_to(f_col, (blk, lanes))
    m_ref[...] = jnp.full(m_ref.shape, NEG, F32)
    l_ref[...] = jnp.zeros(l_ref.shape, F32)
    acc_ref[...] = jnp.zeros(acc_ref.shape, F32)

    def scores(hh, j):
        start = pl.multiple_of(j * blk, blk)
        return _dot_nt(q_ref[0, :, head_lanes[hh]], kb_ref[hh, pl.ds(start, blk), :])

    def softmax(hh, j, s_all, masked):
        f_keys = frow_ref[hh, j]
        p_strips = []
        for r in range(blk // strip):
            rows = slice(r * strip, (r + 1) * strip)
            fq = fq_ref[hh, rows]
            cols = []
            for cc in range(n_col):
                k0 = cc * lanes
                if masked and k0 >= (r + 1) * strip:
                    continue
                s = s_all[rows, k0:k0 + lanes] + fq - f_keys[:, k0:k0 + lanes]
                if masked and k0 + lanes - 1 > r * strip:
                    row = lax.broadcasted_iota(jnp.int32, s.shape, 0) + r * strip
                    col = lax.broadcasted_iota(jnp.int32, s.shape, 1) + k0
                    s = jnp.where(col <= row, s, NEG)
                cols.append(s)
            m_old = m_ref[hh, rows]
            m_blk = functools.reduce(jnp.maximum, cols)
            m_new = jnp.maximum(m_old, jnp.broadcast_to(jnp.max(m_blk, axis=1, keepdims=True), m_old.shape))
            ps = [jnp.exp2(s - m_new) for s in cols]
            a = jnp.exp2(m_old - m_new)
            l_ref[hh, rows] = a * l_ref[hh, rows] + functools.reduce(jnp.add, ps)
            m_ref[hh, rows] = m_new
            acc_ref[hh, rows] = a * acc_ref[hh, rows]
            ps = [p.astype(BF16) for p in ps] + [jnp.zeros((strip, lanes), BF16)] * (n_col - len(ps))
            p_strips.append(jnp.concatenate(ps, axis=1))
        return jnp.concatenate(p_strips, axis=0)

    def values(hh, j, p):
        start = pl.multiple_of(j * blk, blk)
        acc_ref[hh] += _dot(p, vb_ref[hh, pl.ds(start, blk), :])

    def block_steps(blocks):
        chains = [(hh, j, masked) for j, masked in blocks for hh in range(hp)]
        s_next = scores(*chains[0][:2])
        for n, (hh, j, masked) in enumerate(chains):
            s_cur = s_next
            if n + 1 < len(chains):
                s_next = scores(*chains[n + 1][:2])
            values(hh, j, softmax(hh, j, s_cur, masked))

    def pair_body(p, carry):
        block_steps([(2 * p, False), (2 * p + 1, False)])
        return carry

    lax.fori_loop(0, lax.shift_right_logical(qi, 1), pair_body, 0)

    @pl.when((qi & 1) == 1)
    def _():
        block_steps([(qi - 1, False), (qi, True)])

    @pl.when((qi & 1) == 0)
    def _():
        block_steps([(qi, True)])
    for hh in range(hp):
        l = jnp.sum(l_ref[hh], axis=1, keepdims=True)
        o_ref[0, :, head_lanes[hh]] = (acc_ref[hh] / l).astype(o_ref.dtype)


def _fox_prompt(q, k, v, f, *, blk=512, strip=32, hp=4):
    n_b, s, _ = q.shape
    n_blk = s // blk
    n_hg = N_HEADS // hp
    frow = f.reshape(n_b * N_HEADS, n_blk, 1, blk)
    kv_spec = pl.BlockSpec((1, s, hp * D_HEAD), lambda b, h, i: (b, 0, h))
    q_spec = pl.BlockSpec((1, blk, hp * D_HEAD), lambda b, h, i: (b, i, h))
    state = pltpu.VMEM((hp, blk, D_HEAD), F32)
    return pl.pallas_call(
        functools.partial(_fox_prompt_kernel, blk=blk, strip=strip, hp=hp),
        grid=(n_b, n_hg, n_blk),
        in_specs=[q_spec, kv_spec, kv_spec,
                  pl.BlockSpec((hp, n_blk, 1, blk), lambda b, h, i: (b * n_hg + h, 0, 0, 0))],
        out_specs=q_spec,
        out_shape=jax.ShapeDtypeStruct((n_b, s, GROUP_WIDTH), BF16),
        scratch_shapes=[pltpu.VMEM((hp, s, D_HEAD), BF16), pltpu.VMEM((hp, s, D_HEAD), BF16),
                        state, state, state, state],
        compiler_params=_params(3, VMEM_LIMIT_BYTES),
        name="fox_prompt",
    )(q, k, v, frow)


def _band_prompt_kernel(q_ref, k_ref, v_ref, bias_ref, o_ref, kb_ref, vb_ref, *, tb, nq, strip, hp):
    i = pl.program_id(2)
    past = BAND_ROWS
    lanes = D_HEAD
    win = past + tb
    head_lanes = [slice(hh * lanes, (hh + 1) * lanes) for hh in range(hp)]

    @pl.when(i == 0)
    def _():
        for hh in range(hp):
            kb_ref[hh, :past, :] = jnp.zeros((past, D_HEAD), BF16)
            vb_ref[hh, :past, :] = jnp.zeros((past, D_HEAD), BF16)
            kb_ref[hh, past:, :] = k_ref[0, :, head_lanes[hh]]
            vb_ref[hh, past:, :] = v_ref[0, :, head_lanes[hh]]

    chains = [(hh, qq) for qq in range(nq) for hh in range(hp)]

    def block_start(qq):
        return pl.multiple_of((i * nq + qq) * tb, tb)

    def scores(chain):
        hh, qq = chain
        q = q_ref[0, qq * tb:(qq + 1) * tb, head_lanes[hh]]
        return _dot_nt(q, kb_ref[hh, pl.ds(block_start(qq), win), :])

    def softmax(chain, s_all):
        hh, qq = chain
        first_valid = past - block_start(qq)
        p_strips, l_strips = [], []
        for r in range(tb // strip):
            rows = slice(r * strip, (r + 1) * strip)
            q_chunk = (r * strip) // CHUNK
            cols = {}
            for cc in range(win // lanes):
                k0 = cc * lanes
                if k0 < past:
                    if (k0 + lanes - 1) // CHUNK < q_chunk:
                        continue
                    s = s_all[rows, k0:k0 + lanes] + bias_ref[hh, 0, rows, k0:k0 + lanes]
                    col = lax.broadcasted_iota(jnp.int32, s.shape, 1) + k0
                    s = jnp.where(col >= first_valid, s, NEG)
                else:
                    if (k0 - past) // CHUNK > q_chunk:
                        continue
                    s = s_all[rows, k0:k0 + lanes] + bias_ref[hh, 1, rows, k0 - past:k0 - past + lanes]
                cols[cc] = s
            m_blk = functools.reduce(jnp.maximum, cols.values())
            m = jnp.broadcast_to(jnp.max(m_blk, axis=1, keepdims=True), m_blk.shape)
            ps = {cc: jnp.exp2(s - m) for cc, s in cols.items()}
            l_strips.append(functools.reduce(jnp.add, ps.values()))
            zero = jnp.zeros((strip, lanes), BF16)
            p_strips.append(jnp.concatenate(
                [ps[cc].astype(BF16) if cc in ps else zero for cc in range(win // lanes)], axis=1))
        return jnp.concatenate(p_strips, axis=0), jnp.concatenate(l_strips, axis=0)

    def values(chain, p, l_part):
        hh, qq = chain
        acc = _dot(p, vb_ref[hh, pl.ds(block_start(qq), win), :])
        l = jnp.sum(l_part, axis=1, keepdims=True)
        o_ref[0, qq * tb:(qq + 1) * tb, head_lanes[hh]] = (acc / l).astype(o_ref.dtype)

    s_next = scores(chains[0])
    for n, chain in enumerate(chains):
        s_cur = s_next
        if n + 1 < len(chains):
            s_next = scores(chains[n + 1])
        values(chain, *softmax(chain, s_cur))


def _band_prompt(q, k, v, bias, *, tb=BAND_Q_BLOCK, nq=4, strip=32, hp=4):
    n_b, s, _ = q.shape
    kv_spec = pl.BlockSpec((1, s, hp * D_HEAD), lambda h, b, i: (b, 0, h))
    q_spec = pl.BlockSpec((1, nq * tb, hp * D_HEAD), lambda h, b, i: (b, i, h))
    kv_scratch = pltpu.VMEM((hp, BAND_ROWS + s, D_HEAD), BF16)
    return pl.pallas_call(
        functools.partial(_band_prompt_kernel, tb=tb, nq=nq, strip=strip, hp=hp),
        grid=(N_HEADS // hp, n_b, s // (nq * tb)),
        in_specs=[q_spec, kv_spec, kv_spec,
                  pl.BlockSpec((hp, 2, tb, BAND_ROWS), lambda h, b, i: (h, 0, 0, 0))],
        out_specs=q_spec,
        out_shape=jax.ShapeDtypeStruct((n_b, s, GROUP_WIDTH), BF16),
        scratch_shapes=[kv_scratch, kv_scratch],
        compiler_params=_params(3, VMEM_LIMIT_BYTES),
        name="band_prompt",
    )(q, k, v, bias)


def _fox_sample_kernel(q_ref, kc_ref, vc_ref, kn_ref, vn_ref, fc_ref, fn_ref, fq_ref, o_ref,
                       m_ref, l_ref, acc_ref, *, tc):
    c = pl.program_id(1)
    n_new = q_ref.shape[1]
    lanes = D_HEAD

    @pl.when(c == 0)
    def _():
        m_ref[...] = jnp.full(m_ref.shape, NEG, F32)
        l_ref[...] = jnp.zeros(l_ref.shape, F32)
        acc_ref[...] = jnp.zeros(acc_ref.shape, F32)

    def update(h, s, v):
        m_old = m_ref[h][:, :1]
        m_new = jnp.maximum(m_old, jnp.max(s, axis=1, keepdims=True))
        p = jnp.exp2(s - m_new)
        a = jnp.exp2(m_old - m_new)
        l_ref[h] = a * l_ref[h] + jnp.sum(p, axis=1, keepdims=True)
        m_ref[h] = jnp.broadcast_to(m_new, (n_new, lanes))
        acc_ref[h] = a * acc_ref[h] + _dot(p.astype(BF16), v)

    for h in range(N_HEADS):
        hl = slice(h * lanes, (h + 1) * lanes)
        rows = pl.ds(h, tc, stride=N_HEADS)
        s = (_dot_nt(q_ref[0, :, hl], kc_ref[0, rows, :].astype(BF16))
             + fq_ref[0, :, h:h + 1] - fc_ref[0, 0, h:h + 1, :])
        update(h, s, vc_ref[0, rows, :].astype(BF16))

    @pl.when(c == pl.num_programs(1) - 1)
    def _():
        for h in range(N_HEADS):
            hl = slice(h * lanes, (h + 1) * lanes)
            s = (_dot_nt(q_ref[0, :, hl], kn_ref[0, :, hl].astype(BF16))
                 + fq_ref[0, :, h:h + 1] - fn_ref[0, h:h + 1, :])
            row = lax.broadcasted_iota(jnp.int32, s.shape, 0)
            col = lax.broadcasted_iota(jnp.int32, s.shape, 1)
            update(h, jnp.where(col <= row, s, NEG), vn_ref[0, :, hl].astype(BF16))
            o_ref[0, :, hl] = (acc_ref[h] / l_ref[h]).astype(o_ref.dtype)


def _fox_sample(q, k_cache, v_cache, k_new, v_new, f_cache, f_new, f_q, *, tc=2048):
    n_b, n_new, _ = q.shape
    past = k_cache.shape[1] // N_HEADS
    n_chunks = past // tc
    f_cache = jnp.swapaxes(f_cache.reshape(n_b, N_HEADS, n_chunks, tc), 1, 2)
    new_spec = pl.BlockSpec((1, n_new, GROUP_WIDTH), lambda b, c: (b, 0, 0))
    cache_spec = pl.BlockSpec((1, tc * N_HEADS, D_HEAD), lambda b, c: (b, c, 0))
    state = pltpu.VMEM((N_HEADS, n_new, D_HEAD), F32)
    return pl.pallas_call(
        functools.partial(_fox_sample_kernel, tc=tc),
        grid=(n_b, n_chunks),
        in_specs=[new_spec, cache_spec, cache_spec, new_spec, new_spec,
                  pl.BlockSpec((1, 1, N_HEADS, tc), lambda b, c: (b, c, 0, 0)),
                  pl.BlockSpec((1, N_HEADS, n_new), lambda b, c: (b, 0, 0)),
                  pl.BlockSpec((1, n_new, N_HEADS), lambda b, c: (b, 0, 0))],
        out_specs=new_spec,
        out_shape=jax.ShapeDtypeStruct((n_b, n_new, GROUP_WIDTH), BF16),
        scratch_shapes=[state, state, state],
        compiler_params=_params(2, 40 * 1024 * 1024),
        name="fox_sample",
    )(q, k_cache, v_cache, k_new, v_new, f_cache, f_new, f_q)


def _band_sample_kernel(q_ref, kc_ref, vc_ref, kn_ref, vn_ref, bias_ref, o_ref):
    n_new = q_ref.shape[1]
    past = kc_ref.shape[1] // N_HEADS
    lanes = D_HEAD
    for h in range(N_HEADS):
        hl = slice(h * lanes, (h + 1) * lanes)
        rows = pl.ds(h, past, stride=N_HEADS)
        q = q_ref[0, :, hl]
        s_c = _dot_nt(q, kc_ref[0, rows, :].astype(BF16)) + bias_ref[h, 0]
        s_n = _dot_nt(q, kn_ref[0, :, hl].astype(BF16)) + bias_ref[h, 1][:, :n_new]
        m = jnp.maximum(jnp.max(s_c, axis=1, keepdims=True), jnp.max(s_n, axis=1, keepdims=True))
        p_c = jnp.exp2(s_c - m)
        p_n = jnp.exp2(s_n - m)
        l = jnp.sum(p_c, axis=1, keepdims=True) + jnp.sum(p_n, axis=1, keepdims=True)
        acc = (_dot(p_c.astype(BF16), vc_ref[0, rows, :].astype(BF16))
               + _dot(p_n.astype(BF16), vn_ref[0, :, hl].astype(BF16)))
        o_ref[0, :, hl] = (acc / l).astype(o_ref.dtype)


def _band_sample(q, k_cache, v_cache, k_new, v_new, bias):
    n_b, n_new, _ = q.shape
    new_spec = pl.BlockSpec((1, n_new, GROUP_WIDTH), lambda b: (b, 0, 0))
    cache_spec = pl.BlockSpec((1,) + k_cache.shape[1:], lambda b: (b, 0, 0))
    return pl.pallas_call(
        _band_sample_kernel,
        grid=(n_b,),
        in_specs=[new_spec, cache_spec, cache_spec, new_spec, new_spec,
                  pl.BlockSpec((N_HEADS, 2, n_new, BAND_ROWS), lambda b: (0, 0, 0, 0))],
        out_specs=new_spec,
        out_shape=jax.ShapeDtypeStruct((n_b, n_new, GROUP_WIDTH), BF16),
        compiler_params=_params(1, 40 * 1024 * 1024),
        name="band_sample",
    )(q, k_cache, v_cache, k_new, v_new, bias)


def _out_proj_kernel(fox_ref, band_ref, x_ref, ada_ref, w_ref, g_ref, b_ref, o_ref):
    bb, t, d = x_ref.shape
    gw = fox_ref.shape[2]
    for bs, rs, flat in _row_chunks(bb, t):
        x = x_ref[bs, rs, :]
        rows = flat.stop - flat.start
        y = (_dot(fox_ref[bs, rs, :].reshape(rows, gw), w_ref[:gw, :])
             + _dot(band_ref[bs, rs, :].reshape(rows, gw), w_ref[gw:, :]))
        z = ALPHA * x + ada_ref[bs, 2:3, :] * y.reshape(x.shape)
        o_ref[bs, rs, :] = _norm(z) * g_ref[...] + b_ref[...]


def _out_proj(fox, band, x, ada, w_out, g, b, *, bb, t):
    n_b, s, d = x.shape
    tok = lambda w: pl.BlockSpec((bb, t, w), lambda bi, i: (bi, i, 0))
    vec = pl.BlockSpec((1, d), lambda bi, i: (0, 0))
    return pl.pallas_call(
        _out_proj_kernel,
        grid=(n_b // bb, s // t),
        in_specs=[tok(GROUP_WIDTH), tok(GROUP_WIDTH), tok(d),
                  pl.BlockSpec((bb, 6, d), lambda bi, i: (bi, 0, 0)),
                  pl.BlockSpec(w_out.shape, lambda bi, i: (0, 0)), vec, vec],
        out_specs=tok(d),
        out_shape=jax.ShapeDtypeStruct(x.shape, F32),
        compiler_params=_params(2, VMEM_LIMIT_BYTES),
        name="out_proj",
    )(fox, band, x, ada, w_out, g.reshape(1, d), b.reshape(1, d))


def _mlp_kernel(x_ref, ada_ref, wu_ref, bu_ref, wd_ref, g_ref, b_ref, o_ref, u_ref):
    f = pl.program_id(2)
    last = pl.num_programs(2) - 1
    bb, t, d = x_ref.shape
    chunks = _row_chunks(bb, t)

    def hidden(u):
        hid = jnp.maximum(_dot(u, wu_ref[...]) + bu_ref[...], 0.0)
        return (hid * hid).astype(BF16)

    @pl.when(f == 0)
    def _():
        for bs, rs, flat in chunks:
            u = _norm(x_ref[bs, rs, :]) * (1.0 + ada_ref[bs, 4:5, :]) + ada_ref[bs, 3:4, :]
            u = u.reshape(flat.stop - flat.start, d).astype(BF16)
            u_ref[flat, :] = u
            o_ref[bs, rs, :] = _dot(hidden(u), wd_ref[...]).reshape(x_ref[bs, rs, :].shape)

    @pl.when(jnp.logical_and(f > 0, f < last))
    def _():
        o_ref[...] += _dot(hidden(u_ref[...]), wd_ref[...]).reshape(bb, t, d)

    @pl.when(f == last)
    def _():
        for bs, rs, flat in chunks:
            x = x_ref[bs, rs, :]
            y = o_ref[bs, rs, :] + _dot(hidden(u_ref[flat, :]), wd_ref[...]).reshape(x.shape)
            z = ALPHA * x + ada_ref[bs, 5:6, :] * y
            o_ref[bs, rs, :] = _norm(z) * g_ref[...] + b_ref[...]


def _mlp(x, ada, w_up, b_up, w_down, g, b, *, bb, t, tf=1024):
    n_b, s, d = x.shape
    d_ff = w_up.shape[1]
    assert d_ff // tf >= 2, "first and last hidden tiles are separate steps"
    tok = pl.BlockSpec((bb, t, d), lambda bi, i, f: (bi, i, 0))
    vec = pl.BlockSpec((1, d), lambda bi, i, f: (0, 0))
    return pl.pallas_call(
        _mlp_kernel,
        grid=(n_b // bb, s // t, d_ff // tf),
        in_specs=[tok, pl.BlockSpec((bb, 6, d), lambda bi, i, f: (bi, 0, 0)),
                  pl.BlockSpec((d, tf), lambda bi, i, f: (0, f)),
                  pl.BlockSpec((1, tf), lambda bi, i, f: (0, f)),
                  pl.BlockSpec((tf, d), lambda bi, i, f: (f, 0)), vec, vec],
        out_specs=tok,
        out_shape=jax.ShapeDtypeStruct(x.shape, F32),
        scratch_shapes=[pltpu.VMEM((bb * t, d), BF16)],
        compiler_params=_params(3, VMEM_LIMIT_BYTES),
        name="mlp",
    )(x, ada, w_up, b_up.reshape(1, d_ff), w_down, g.reshape(1, d), b.reshape(1, d))


def kernel(x_prompt, x_sample, cache_fox_k, cache_fox_v, cache_fox_logf, cache_band_k, cache_band_v,
           c_prompt, c_sample, w_ada, b_ada, w_in, b_forget, rel_bias, w_out, ln_mix_g, ln_mix_b,
           w_up, b_up, w_down, ln_mlp_g, ln_mlp_b):
    assert w_in.shape[0] == 1, "single-layer step"
    n_b, seq, d = x_prompt.shape
    n_bs, n_new, _ = x_sample.shape
    past = cache_fox_k.shape[2]
    gw = GROUP_WIDTH

    w = w_in[0]
    w_fox = w[:, :3 * gw].astype(BF16)
    w_band = w[:, 3 * gw + N_HEADS:].astype(BF16)
    wf_t = jnp.pad(w[:, 3 * gw:3 * gw + N_HEADS].T, ((0, 16 - N_HEADS), (0, 0))).astype(BF16)
    w_out_b = w_out[0].astype(BF16)
    w_up_b = w_up[0].astype(BF16)
    w_down_b = w_down[0].astype(BF16)

    ada = _ada(jnp.concatenate([c_prompt, c_sample], axis=0), w_ada[0], b_ada[0])
    ada = ada.reshape(n_b + n_bs, 6, d)
    ada_p, ada_s = ada[:n_b], ada[n_b:]
    bias = _band_bias(rel_bias[0], BAND_Q_BLOCK)

    qa, ka, va, qb, kb, vb, kb_tail, vb_tail, lft = _in_proj(
        x_prompt, ada_p, w_fox, w_band, wf_t, b_forget[0], bb=1, t=1024)
    f_p = _cumsum_lanes(lft)
    fox = _fox_prompt(qa, ka, va, f_p)
    band = _band_prompt(qb, kb, vb, bias)
    x1 = _out_proj(fox, band, x_prompt, ada_p, w_out_b, ln_mix_g[0], ln_mix_b[0], bb=1, t=1024)
    y_p = _mlp(x1, ada_p, w_up_b, b_up[0], w_down_b, ln_mlp_g[0], ln_mlp_b[0], bb=1, t=1024)

    heads = lambda a: a.reshape(1, a.shape[0], a.shape[1], N_HEADS, D_HEAD)
    p_states = (heads(ka), heads(va), jnp.swapaxes(lft, 1, 2)[None], heads(kb_tail), heads(vb_tail))

    qa_s, ka_s, va_s, qb_s, kb_s, vb_s, kb_tail_s, vb_tail_s, lft_s = _in_proj(
        x_sample, ada_s, w_fox, w_band, wf_t, b_forget[0], bb=n_bs, t=n_new)
    logf_s = jnp.transpose(lft_s.reshape(N_HEADS, n_bs, n_new), (1, 0, 2))
    total = past + n_new
    padded = -(-total // 128) * 128
    logf_all = jnp.concatenate(
        [jnp.swapaxes(cache_fox_logf[0], 1, 2), logf_s, jnp.zeros((n_bs, N_HEADS, padded - total), F32)], axis=2)
    f_s = _cumsum_lanes(logf_all)
    f_new = f_s[:, :, past:total]
    rows_view = lambda cache: cache.reshape(n_bs, cache.shape[2] * N_HEADS, D_HEAD)
    fox_s = _fox_sample(qa_s, rows_view(cache_fox_k), rows_view(cache_fox_v), ka_s, va_s,
                        f_s[:, :, :past], f_new, jnp.swapaxes(f_new, 1, 2))
    band_s = _band_sample(qb_s, rows_view(cache_band_k), rows_view(cache_band_v), kb_s, vb_s, bias)
    x1_s = _out_proj(fox_s, band_s, x_sample, ada_s, w_out_b, ln_mix_g[0], ln_mix_b[0], bb=n_bs, t=n_new)
    y_s = _mlp(x1_s, ada_s, w_up_b, b_up[0], w_down_b, ln_mlp_g[0], ln_mlp_b[0], bb=n_bs, t=n_new)
    s_states = (heads(ka_s), heads(va_s), jnp.swapaxes(logf_s, 1, 2)[None],
                heads(kb_tail_s), heads(vb_tail_s))

    return (y_p, y_s) + p_states + s_states
```

```python
import functools
import math

import jax
import jax.numpy as jnp
from jax import lax
from jax.experimental import pallas as pl
from jax.experimental.pallas import tpu as pltpu

D_HEAD = 128
N_HEADS = 8
GROUP_WIDTH = N_HEADS * D_HEAD
CHUNK = 64
BAND_ROWS = 512
BAND_Q_BLOCK = 256
MAX_REL = 128
N_REL = 2 * MAX_REL + 1
ALPHA = 2.0 ** 0.25
LN_EPS = 1e-5
NEG = -1e30
LOG2E = math.log2(math.e)
Q_SCALE = D_HEAD ** -0.5 * LOG2E

VMEM_LIMIT_BYTES = 60 * 1024 * 1024

F32 = jnp.float32
BF16 = jnp.bfloat16


def _params(n_axes, vmem=None):
    return pltpu.CompilerParams(dimension_semantics=("arbitrary",) * n_axes, vmem_limit_bytes=vmem)


def _norm(x):
    mu = jnp.mean(x, axis=-1, keepdims=True)
    xc = x - mu
    var = jnp.mean(xc * xc, axis=-1, keepdims=True)
    return xc * lax.rsqrt(var + LN_EPS)


def _dot_nt(a, b):
    return lax.dot_general(a, b, (((1,), (1,)), ((), ())), preferred_element_type=F32)


def _dot(a, b):
    return jnp.dot(a, b, preferred_element_type=F32)


ROW_CHUNK = 256


def _row_chunks(bb, t):
    if t >= ROW_CHUNK:
        return [(slice(b, b + 1), slice(r, r + ROW_CHUNK), slice(b * t + r, b * t + r + ROW_CHUNK))
                for b in range(bb) for r in range(0, t, ROW_CHUNK)]
    nb = min(ROW_CHUNK // t, bb)
    return [(slice(b, b + nb), slice(0, t), slice(b * t, (b + nb) * t)) for b in range(0, bb, nb)]


def _ada_kernel(c_ref, w_ref, b_ref, o_ref):
    a = jax.nn.silu(c_ref[...]).astype(BF16)
    o_ref[...] = _dot(a, w_ref[...].astype(BF16)) + b_ref[...]


def _ada(c, w_ada, b_ada, tn=1024):
    n_rows, d = c.shape
    n_out = w_ada.shape[1]
    return pl.pallas_call(
        _ada_kernel,
        grid=(n_out // tn,),
        in_specs=[pl.BlockSpec((n_rows, d), lambda j: (0, 0)),
                  pl.BlockSpec((d, tn), lambda j: (0, j)),
                  pl.BlockSpec((1, tn), lambda j: (0, j))],
        out_specs=pl.BlockSpec((n_rows, tn), lambda j: (0, j)),
        out_shape=jax.ShapeDtypeStruct((n_rows, n_out), F32),
        compiler_params=_params(1, 40 * 1024 * 1024),
        name="ada",
    )(c, w_ada, b_ada.reshape(1, n_out))


def _band_bias_kernel(tab_ref, o_ref):
    h = pl.program_id(0)
    t = BAND_ROWS
    w = 2 * t
    kidx = lax.broadcasted_iota(jnp.int32, (1, w), 1)
    d = jnp.where(kidx < t, kidx, kidx - w)
    idx_cur = jnp.clip(d, -MAX_REL, MAX_REL) + MAX_REL
    idx_prev = jnp.clip(d - t, -MAX_REL, MAX_REL) + MAX_REL

    def fill(r, rows):
        rc, rp = rows
        val = tab_ref[h, r] * LOG2E
        return jnp.where(idx_cur == r, val, rc), jnp.where(idx_prev == r, val, rp)

    rc, rp = lax.fori_loop(0, N_REL, fill, (jnp.zeros((1, w), F32), jnp.zeros((1, w), F32)))
    n_q = o_ref.shape[2]
    qrow = lax.broadcasted_iota(jnp.int32, (n_q, w), 0)

    def toeplitz(row):
        x = jnp.broadcast_to(row, (n_q, w))
        for bit in range(n_q.bit_length() - 1):
            x = jnp.where(((qrow >> bit) & 1) == 1, pltpu.roll(x, 1 << bit, 1), x)
        return x[:, :t]

    qc = lax.broadcasted_iota(jnp.int32, (n_q, t), 0) // CHUNK
    kc = lax.broadcasted_iota(jnp.int32, (n_q, t), 1) // CHUNK
    o_ref[0, 1] = jnp.where(kc <= qc, toeplitz(rc), NEG)
    o_ref[0, 0] = jnp.where(kc >= qc, toeplitz(rp), NEG)


def _band_bias(table, n_q):
    t = BAND_ROWS
    assert n_q & (n_q - 1) == 0 and n_q <= t
    return pl.pallas_call(
        _band_bias_kernel,
        grid=(N_HEADS,),
        in_specs=[pl.BlockSpec(memory_space=pltpu.SMEM)],
        out_specs=pl.BlockSpec((1, 2, n_q, t), lambda h: (h, 0, 0, 0)),
        out_shape=jax.ShapeDtypeStruct((N_HEADS, 2, n_q, t), F32),
        compiler_params=_params(1, 40 * 1024 * 1024),
        name="band_bias",
    )(table)


def _in_proj_kernel(x_ref, ada_ref, wfox_ref, wband_ref, wf_ref, bf_ref,
                    qa_ref, ka_ref, va_ref, qb_ref, kb_ref, vb_ref, kbt_ref, vbt_ref, lf_ref, u_ref,
                    *, tiles_per_out):
    j = pl.program_id(2)
    bb, t, d = x_ref.shape
    tn = wfox_ref.shape[1]
    tail = kbt_ref.shape[1]
    in_tail_block = pl.program_id(1) == pl.num_programs(1) - 1

    @pl.when(j == 0)
    def _():
        for bs, rs, flat in _row_chunks(bb, t):
            x = x_ref[bs, rs, :]
            u = _norm(x) * (1.0 + ada_ref[bs, 1:2, :]) + ada_ref[bs, 0:1, :]
            u = u.reshape(flat.stop - flat.start, d).astype(BF16)
            u_ref[flat, :] = u
            qa_ref[bs, rs, :] = (_dot(u, wfox_ref[...]) * Q_SCALE).astype(qa_ref.dtype).reshape(x.shape[:2] + (tn,))
        f = _dot_nt(wf_ref[...], u_ref[...])
        lf_ref[0] = jax.nn.log_sigmoid(f[:N_HEADS] + bf_ref[...])

    outs = ((qa_ref, Q_SCALE, None), (ka_ref, None, None), (va_ref, None, None),
            (qb_ref, Q_SCALE, None), (kb_ref, None, kbt_ref), (vb_ref, None, vbt_ref))
    for o, (ref, scale, tail_ref) in enumerate(outs):
        @pl.when(jnp.logical_and(j // tiles_per_out == o, j > 0))
        def _(ref=ref, scale=scale, tail_ref=tail_ref, w_ref=wfox_ref if o < 3 else wband_ref):
            acc = _dot(u_ref[...], w_ref[...])
            val = acc if scale is None else acc * scale
            ref[...] = val.astype(ref.dtype).reshape(bb, t, tn)
            if tail_ref is not None:
                @pl.when(in_tail_block)
                def _():
                    tail_ref[...] = acc.reshape(bb, t, tn)[:, t - tail:, :]


def _in_proj(x, ada, w_fox, w_band, wf_t, b_forget, *, bb, t, tn=512):
    n_b, s, d = x.shape
    m = bb * t
    tiles_per_out = GROUP_WIDTH // tn
    n_j = 6 * tiles_per_out

    n_i = s // t

    def out_spec(o):
        def index(b, i, j):
            return b, i, jnp.clip(j - o * tiles_per_out, 0, tiles_per_out - 1)
        return pl.BlockSpec((bb, t, tn), index)

    tail = min(BAND_ROWS, s)

    def tail_spec(o):
        def index(b, i, j):
            col = jnp.where(i == n_i - 1, jnp.clip(j - o * tiles_per_out, 0, tiles_per_out - 1), 0)
            return b, 0, col
        return pl.BlockSpec((bb, tail, tn), index)

    tok = lambda dt: jax.ShapeDtypeStruct((n_b, s, GROUP_WIDTH), dt)
    tail_shape = jax.ShapeDtypeStruct((n_b, tail, GROUP_WIDTH), F32)
    return pl.pallas_call(
        functools.partial(_in_proj_kernel, tiles_per_out=tiles_per_out),
        grid=(n_b // bb, s // t, n_j),
        in_specs=[pl.BlockSpec((bb, t, d), lambda b, i, j: (b, i, 0)),
                  pl.BlockSpec((bb, 6, d), lambda b, i, j: (b, 0, 0)),
                  pl.BlockSpec((d, tn), lambda b, i, j: (0, jnp.minimum(j, n_j // 2 - 1))),
                  pl.BlockSpec((d, tn), lambda b, i, j: (0, jnp.maximum(j - n_j // 2, 0))),
                  pl.BlockSpec(wf_t.shape, lambda b, i, j: (0, 0)),
                  pl.BlockSpec((N_HEADS, 1), lambda b, i, j: (0, 0))],
        out_specs=[out_spec(o) for o in range(6)] + [tail_spec(4), tail_spec(5)]
        + [pl.BlockSpec((1, N_HEADS, m), lambda b, i, j: (b, 0, i))],
        out_shape=[tok(BF16), tok(F32), tok(F32), tok(BF16), tok(BF16), tok(BF16), tail_shape, tail_shape,
                   jax.ShapeDtypeStruct((n_b // bb, N_HEADS, (s // t) * m), F32)],
        scratch_shapes=[pltpu.VMEM((m, d), BF16)],
        compiler_params=_params(3, VMEM_LIMIT_BYTES),
        name="in_proj",
    )(x, ada, w_fox, w_band, wf_t, b_forget.reshape(N_HEADS, 1))


def _cumsum_kernel(x_ref, o_ref):
    x = x_ref[0]
    n = x.shape[1]
    lane = lax.broadcasted_iota(jnp.int32, x.shape, 1)
    shift = 1
    while shift < n:
        x = x + jnp.where(lane >= shift, pltpu.roll(x, shift, 1), 0.0)
        shift *= 2
    o_ref[0] = x * LOG2E


def _cumsum_lanes(x):
    n_b, h, n = x.shape
    spec = pl.BlockSpec((1, h, n), lambda b: (b, 0, 0))
    return pl.pallas_call(
        _cumsum_kernel, grid=(n_b,), in_specs=[spec], out_specs=spec,
        out_shape=jax.ShapeDtypeStruct(x.shape, F32),
        compiler_params=_params(1), name="cumsum",
    )(x)


def _fox_prompt_kernel(q_ref, k_ref, v_ref, frow_ref, o_ref,
                       kb_ref, vb_ref, fq_ref, m_ref, l_ref, acc_ref, *, blk, strip, hp):
    qi = pl.program_id(2)
    lanes = D_HEAD
    n_col = blk // lanes
    head_lanes = [slice(hh * lanes, (hh + 1) * lanes) for hh in range(hp)]

    @pl.when(qi == 0)
    def _():
        for hh in range(hp):
            kb_ref[hh] = k_ref[0, :, head_lanes[hh]].astype(BF16)
            vb_ref[hh] = v_ref[0, :, head_lanes[hh]].astype(BF16)

    on_diag = (lax.broadcasted_iota(jnp.int32, (blk, blk), 0) == lax.broadcasted_iota(jnp.int32, (blk, blk), 1))
    for hh in range(hp):
        f_col = jnp.sum(jnp.where(on_diag, frow_ref[hh, qi], 0.0), axis=1, keepdims=True)
        fq_ref[hh] = jnp.broadcast_to(f_col, (blk, lanes))
    m_ref[...] = jnp.full(m_ref.shape, NEG, F32)
    l_ref[...] = jnp.zeros(l_ref.shape, F32)
    acc_ref[...] = jnp.zeros(acc_ref.shape, F32)

    def scores(hh, j):
        start = pl.multiple_of(j * blk, blk)
        return _dot_nt(q_ref[0, :, head_lanes[hh]], kb_ref[hh, pl.ds(start, blk), :])

    def softmax(hh, j, s_all, masked):
        f_keys = frow_ref[hh, j]
        p_strips = []
        for r in range(blk // strip):
            rows = slice(r * strip, (r + 1) * strip)
            fq = fq_ref[hh, rows]
            cols = []
            for cc in range(n_col):
                k0 = cc * lanes
                if masked and k0 >= (r + 1) * strip:
                    continue
                s = s_all[rows, k0:k0 + lanes] + fq - f_keys[:, k0:k0 + lanes]
                if masked and k0 + lanes - 1 > r * strip:
                    row = lax.broadcasted_iota(jnp.int32, s.shape, 0) + r * strip
                    col = lax.broadcasted_iota(jnp.int32, s.shape, 1) + k0
                    s = jnp.where(col <= row, s, NEG)
                cols.append(s)
            m_old = m_ref[hh, rows]
            m_blk = functools.reduce(jnp.maximum, cols)
            m_new = jnp.maximum(m_old, jnp.broadcast_to(jnp.max(m_blk, axis=1, keepdims=True), m_old.shape))
            ps = [jnp.exp2(s - m_new) for s in cols]
            a = jnp.exp2(m_old - m_new)
            l_ref[hh, rows] = a * l_ref[hh, rows] + functools.reduce(jnp.add, ps)
            m_ref[hh, rows] = m_new
            acc_ref[hh, rows] = a * acc_ref[hh, rows]
            ps = [p.astype(BF16) for p in ps] + [jnp.zeros((strip, lanes), BF16)] * (n_col - len(ps))
            p_strips.append(jnp.concatenate(ps, axis=1))
        return jnp.concatenate(p_strips, axis=0)

    def values(hh, j, p):
        start = pl.multiple_of(j * blk, blk)
        acc_ref[hh] += _dot(p, vb_ref[hh, pl.ds(start, blk), :])

    def block_steps(blocks):
        chains = [(hh, j, masked) for j, masked in blocks for hh in range(hp)]
        s_next = scores(*chains[0][:2])
        for n, (hh, j, masked) in enumerate(chains):
            s_cur = s_next
            if n + 1 < len(chains):
                s_next = scores(*chains[n + 1][:2])
            values(hh, j, softmax(hh, j, s_cur, masked))

    def pair_body(p, carry):
        block_steps([(2 * p, False), (2 * p + 1, False)])
        return carry

    lax.fori_loop(0, lax.shift_right_logical(qi, 1), pair_body, 0)

    @pl.when((qi & 1) == 1)
    def _():
        block_steps([(qi - 1, False)])

    block_steps([(qi, True)])
    for hh in range(hp):
        l = jnp.sum(l_ref[hh], axis=1, keepdims=True)
        o_ref[0, :, head_lanes[hh]] = (acc_ref[hh] / l).astype(o_ref.dtype)


def _fox_prompt(q, k, v, f, *, blk=512, strip=32, hp=4):
    n_b, s, _ = q.shape
    n_blk = s // blk
    n_hg = N_HEADS // hp
    frow = f.reshape(n_b * N_HEADS, n_blk, 1, blk)
    kv_spec = pl.BlockSpec((1, s, hp * D_HEAD), lambda b, h, i: (b, 0, h))
    q_spec = pl.BlockSpec((1, blk, hp * D_HEAD), lambda b, h, i: (b, i, h))
    state = pltpu.VMEM((hp, blk, D_HEAD), F32)
    return pl.pallas_call(
        functools.partial(_fox_prompt_kernel, blk=blk, strip=strip, hp=hp),
        grid=(n_b, n_hg, n_blk),
        in_specs=[q_spec, kv_spec, kv_spec,
                  pl.BlockSpec((hp, n_blk, 1, blk), lambda b, h, i: (b * n_hg + h, 0, 0, 0))],
        out_specs=q_spec,
        out_shape=jax.ShapeDtypeStruct((n_b, s, GROUP_WIDTH), BF16),
        scratch_shapes=[pltpu.VMEM((hp, s, D_HEAD), BF16), pltpu.VMEM((hp, s, D_HEAD), BF16),
                        state, state, state, state],
        compiler_params=_params(3, VMEM_LIMIT_BYTES),
        name="fox_prompt",
    )(q, k, v, frow)


def _band_prompt_kernel(q_ref, k_ref, v_ref, bias_ref, o_ref, kb_ref, vb_ref, *, tb, nq, strip, hp):
    i = pl.program_id(2)
    past = BAND_ROWS
    lanes = D_HEAD
    win = past + tb
    head_lanes = [slice(hh * lanes, (hh + 1) * lanes) for hh in range(hp)]

    @pl.when(i == 0)
    def _():
        for hh in range(hp):
            kb_ref[hh, :past, :] = jnp.zeros((past, D_HEAD), BF16)
            vb_ref[hh, :past, :] = jnp.zeros((past, D_HEAD), BF16)
            kb_ref[hh, past:, :] = k_ref[0, :, head_lanes[hh]]
            vb_ref[hh, past:, :] = v_ref[0, :, head_lanes[hh]]

    chains = [(hh, qq) for qq in range(nq) for hh in range(hp)]

    def block_start(qq):
        return pl.multiple_of((i * nq + qq) * tb, tb)

    def scores(chain):
        hh, qq = chain
        q = q_ref[0, qq * tb:(qq + 1) * tb, head_lanes[hh]]
        return _dot_nt(q, kb_ref[hh, pl.ds(block_start(qq), win), :])

    def softmax(chain, s_all):
        hh, qq = chain
        first_valid = past - block_start(qq)
        p_strips, l_strips = [], []
        for r in range(tb // strip):
            rows = slice(r * strip, (r + 1) * strip)
            q_chunk = (r * strip) // CHUNK
            cols = {}
            for cc in range(win // lanes):
                k0 = cc * lanes
                if k0 < past:
                    if (k0 + lanes - 1) // CHUNK < q_chunk:
                        continue
                    s = s_all[rows, k0:k0 + lanes] + bias_ref[hh, 0, rows, k0:k0 + lanes]
                    col = lax.broadcasted_iota(jnp.int32, s.shape, 1) + k0
                    s = jnp.where(col >= first_valid, s, NEG)
                else:
                    if (k0 - past) // CHUNK > q_chunk:
                        continue
                    s = s_all[rows, k0:k0 + lanes] + bias_ref[hh, 1, rows, k0 - past:k0 - past + lanes]
                cols[cc] = s
            m_blk = functools.reduce(jnp.maximum, cols.values())
            m = jnp.broadcast_to(jnp.max(m_blk, axis=1, keepdims=True), m_blk.shape)
            ps = {cc: jnp.exp2(s - m) for cc, s in cols.items()}
            l_strips.append(functools.reduce(jnp.add, ps.values()))
            zero = jnp.zeros((strip, lanes), BF16)
            p_strips.append(jnp.concatenate(
                [ps[cc].astype(BF16) if cc in ps else zero for cc in range(win // lanes)], axis=1))
        return jnp.concatenate(p_strips, axis=0), jnp.concatenate(l_strips, axis=0)

    def values(chain, p, l_part):
        hh, qq = chain
        acc = _dot(p, vb_ref[hh, pl.ds(block_start(qq), win), :])
        l = jnp.sum(l_part, axis=1, keepdims=True)
        o_ref[0, qq * tb:(qq + 1) * tb, head_lanes[hh]] = (acc / l).astype(o_ref.dtype)

    s_next = scores(chains[0])
    for n, chain in enumerate(chains):
        s_cur = s_next
        if n + 1 < len(chains):
            s_next = scores(chains[n + 1])
        values(chain, *softmax(chain, s_cur))


def _band_prompt(q, k, v, bias, *, tb=BAND_Q_BLOCK, nq=4, strip=32, hp=4):
    n_b, s, _ = q.shape
    kv_spec = pl.BlockSpec((1, s, hp * D_HEAD), lambda h, b, i: (b, 0, h))
    q_spec = pl.BlockSpec((1, nq * tb, hp * D_HEAD), lambda h, b, i: (b, i, h))
    kv_scratch = pltpu.VMEM((hp, BAND_ROWS + s, D_HEAD), BF16)
    return pl.pallas_call(
        functools.partial(_band_prompt_kernel, tb=tb, nq=nq, strip=strip, hp=hp),
        grid=(N_HEADS // hp, n_b, s // (nq * tb)),
        in_specs=[q_spec, kv_spec, kv_spec,
                  pl.BlockSpec((hp, 2, tb, BAND_ROWS), lambda h, b, i: (h, 0, 0, 0))],
        out_specs=q_spec,
        out_shape=jax.ShapeDtypeStruct((n_b, s, GROUP_WIDTH), BF16),
        scratch_shapes=[kv_scratch, kv_scratch],
        compiler_params=_params(3, VMEM_LIMIT_BYTES),
        name="band_prompt",
    )(q, k, v, bias)


def _fox_sample_kernel(q_ref, kc_ref, vc_ref, kn_ref, vn_ref, fc_ref, fn_ref, fq_ref, o_ref,
                       m_ref, l_ref, acc_ref, *, tc):
    c = pl.program_id(1)
    n_new = q_ref.shape[1]
    lanes = D_HEAD

    @pl.when(c == 0)
    def _():
        m_ref[...] = jnp.full(m_ref.shape, NEG, F32)
        l_ref[...] = jnp.zeros(l_ref.shape, F32)
        acc_ref[...] = jnp.zeros(acc_ref.shape, F32)

    def update(h, s, v):
        m_old = m_ref[h][:, :1]
        m_new = jnp.maximum(m_old, jnp.max(s, axis=1, keepdims=True))
        p = jnp.exp2(s - m_new)
        a = jnp.exp2(m_old - m_new)
        l_ref[h] = a * l_ref[h] + jnp.sum(p, axis=1, keepdims=True)
        m_ref[h] = jnp.broadcast_to(m_new, (n_new, lanes))
        acc_ref[h] = a * acc_ref[h] + _dot(p.astype(BF16), v)

    for h in range(N_HEADS):
        hl = slice(h * lanes, (h + 1) * lanes)
        rows = pl.ds(h, tc, stride=N_HEADS)
        s = (_dot_nt(q_ref[0, :, hl], kc_ref[0, rows, :].astype(BF16))
             + fq_ref[0, :, h:h + 1] - fc_ref[0, 0, h:h + 1, :])
        update(h, s, vc_ref[0, rows, :].astype(BF16))

    @pl.when(c == pl.num_programs(1) - 1)
    def _():
        for h in range(N_HEADS):
            hl = slice(h * lanes, (h + 1) * lanes)
            s = (_dot_nt(q_ref[0, :, hl], kn_ref[0, :, hl].astype(BF16))
                 + fq_ref[0, :, h:h + 1] - fn_ref[0, h:h + 1, :])
            row = lax.broadcasted_iota(jnp.int32, s.shape, 0)
            col = lax.broadcasted_iota(jnp.int32, s.shape, 1)
            update(h, jnp.where(col <= row, s, NEG), vn_ref[0, :, hl].astype(BF16))
            o_ref[0, :, hl] = (acc_ref[h] / l_ref[h]).astype(o_ref.dtype)


def _fox_sample(q, k_cache, v_cache, k_new, v_new, f_cache, f_new, f_q, *, tc=2048):
    n_b, n_new, _ = q.shape
    past = k_cache.shape[1] // N_HEADS
    n_chunks = past // tc
    f_cache = jnp.swapaxes(f_cache.reshape(n_b, N_HEADS, n_chunks, tc), 1, 2)
    new_spec = pl.BlockSpec((1, n_new, GROUP_WIDTH), lambda b, c: (b, 0, 0))
    cache_spec = pl.BlockSpec((1, tc * N_HEADS, D_HEAD), lambda b, c: (b, c, 0))
    state = pltpu.VMEM((N_HEADS, n_new, D_HEAD), F32)
    return pl.pallas_call(
        functools.partial(_fox_sample_kernel, tc=tc),
        grid=(n_b, n_chunks),
        in_specs=[new_spec, cache_spec, cache_spec, new_spec, new_spec,
                  pl.BlockSpec((1, 1, N_HEADS, tc), lambda b, c: (b, c, 0, 0)),
                  pl.BlockSpec((1, N_HEADS, n_new), lambda b, c: (b, 0, 0)),
                  pl.BlockSpec((1, n_new, N_HEADS), lambda b, c: (b, 0, 0))],
        out_specs=new_spec,
        out_shape=jax.ShapeDtypeStruct((n_b, n_new, GROUP_WIDTH), BF16),
        scratch_shapes=[state, state, state],
        compiler_params=_params(2, 40 * 1024 * 1024),
        name="fox_sample",
    )(q, k_cache, v_cache, k_new, v_new, f_cache, f_new, f_q)


def _band_sample_kernel(q_ref, kc_ref, vc_ref, kn_ref, vn_ref, bias_ref, o_ref):
    n_new = q_ref.shape[1]
    past = kc_ref.shape[1] // N_HEADS
    lanes = D_HEAD
    for h in range(N_HEADS):
        hl = slice(h * lanes, (h + 1) * lanes)
        rows = pl.ds(h, past, stride=N_HEADS)
        q = q_ref[0, :, hl]
        s_c = _dot_nt(q, kc_ref[0, rows, :].astype(BF16)) + bias_ref[h, 0]
        s_n = _dot_nt(q, kn_ref[0, :, hl].astype(BF16)) + bias_ref[h, 1][:, :n_new]
        m = jnp.maximum(jnp.max(s_c, axis=1, keepdims=True), jnp.max(s_n, axis=1, keepdims=True))
        p_c = jnp.exp2(s_c - m)
        p_n = jnp.exp2(s_n - m)
        l = jnp.sum(p_c, axis=1, keepdims=True) + jnp.sum(p_n, axis=1, keepdims=True)
        acc = (_dot(p_c.astype(BF16), vc_ref[0, rows, :].astype(BF16))
               + _dot(p_n.astype(BF16), vn_ref[0, :, hl].astype(BF16)))
        o_ref[0, :, hl] = (acc / l).astype(o_ref.dtype)


def _band_sample(q, k_cache, v_cache, k_new, v_new, bias):
    n_b, n_new, _ = q.shape
    new_spec = pl.BlockSpec((1, n_new, GROUP_WIDTH), lambda b: (b, 0, 0))
    cache_spec = pl.BlockSpec((1,) + k_cache.shape[1:], lambda b: (b, 0, 0))
    return pl.pallas_call(
        _band_sample_kernel,
        grid=(n_b,),
        in_specs=[new_spec, cache_spec, cache_spec, new_spec, new_spec,
                  pl.BlockSpec((N_HEADS, 2, n_new, BAND_ROWS), lambda b: (0, 0, 0, 0))],
        out_specs=new_spec,
        out_shape=jax.ShapeDtypeStruct((n_b, n_new, GROUP_WIDTH), BF16),
        compiler_params=_params(1, 40 * 1024 * 1024),
        name="band_sample",
    )(q, k_cache, v_cache, k_new, v_new, bias)


def _out_proj_kernel(fox_ref, band_ref, x_ref, ada_ref, w_ref, g_ref, b_ref, o_ref):
    bb, t, d = x_ref.shape
    gw = fox_ref.shape[2]
    for bs, rs, flat in _row_chunks(bb, t):
        x = x_ref[bs, rs, :]
        rows = flat.stop - flat.start
        y = (_dot(fox_ref[bs, rs, :].reshape(rows, gw), w_ref[:gw, :])
             + _dot(band_ref[bs, rs, :].reshape(rows, gw), w_ref[gw:, :]))
        z = ALPHA * x + ada_ref[bs, 2:3, :] * y.reshape(x.shape)
        o_ref[bs, rs, :] = _norm(z) * g_ref[...] + b_ref[...]


def _out_proj(fox, band, x, ada, w_out, g, b, *, bb, t):
    n_b, s, d = x.shape
    tok = lambda w: pl.BlockSpec((bb, t, w), lambda bi, i: (bi, i, 0))
    vec = pl.BlockSpec((1, d), lambda bi, i: (0, 0))
    return pl.pallas_call(
        _out_proj_kernel,
        grid=(n_b // bb, s // t),
        in_specs=[tok(GROUP_WIDTH), tok(GROUP_WIDTH), tok(d),
                  pl.BlockSpec((bb, 6, d), lambda bi, i: (bi, 0, 0)),
                  pl.BlockSpec(w_out.shape, lambda bi, i: (0, 0)), vec, vec],
        out_specs=tok(d),
        out_shape=jax.ShapeDtypeStruct(x.shape, F32),
        compiler_params=_params(2, VMEM_LIMIT_BYTES),
        name="out_proj",
    )(fox, band, x, ada, w_out, g.reshape(1, d), b.reshape(1, d))


def _mlp_kernel(x_ref, ada_ref, wu_ref, bu_ref, wd_ref, g_ref, b_ref, o_ref, u_ref):
    f = pl.program_id(2)
    last = pl.num_programs(2) - 1
    bb, t, d = x_ref.shape
    chunks = _row_chunks(bb, t)

    def hidden(u):
        hid = jnp.maximum(_dot(u, wu_ref[...]) + bu_ref[...], 0.0)
        return (hid * hid).astype(BF16)

    @pl.when(f == 0)
    def _():
        for bs, rs, flat in chunks:
            u = _norm(x_ref[bs, rs, :]) * (1.0 + ada_ref[bs, 4:5, :]) + ada_ref[bs, 3:4, :]
            u = u.reshape(flat.stop - flat.start, d).astype(BF16)
            u_ref[flat, :] = u
            o_ref[bs, rs, :] = _dot(hidden(u), wd_ref[...]).reshape(x_ref[bs, rs, :].shape)

    @pl.when(jnp.logical_and(f > 0, f < last))
    def _():
        o_ref[...] += _dot(hidden(u_ref[...]), wd_ref[...]).reshape(bb, t, d)

    @pl.when(f == last)
    def _():
        for bs, rs, flat in chunks:
            x = x_ref[bs, rs, :]
            y = o_ref[bs, rs, :] + _dot(hidden(u_ref[flat, :]), wd_ref[...]).reshape(x.shape)
            z = ALPHA * x + ada_ref[bs, 5:6, :] * y
            o_ref[bs, rs, :] = _norm(z) * g_ref[...] + b_ref[...]


def _mlp(x, ada, w_up, b_up, w_down, g, b, *, bb, t, tf=1024):
    n_b, s, d = x.shape
    d_ff = w_up.shape[1]
    assert d_ff // tf >= 2, "first and last hidden tiles are separate steps"
    tok = pl.BlockSpec((bb, t, d), lambda bi, i, f: (bi, i, 0))
    vec = pl.BlockSpec((1, d), lambda bi, i, f: (0, 0))
    return pl.pallas_call(
        _mlp_kernel,
        grid=(n_b // bb, s // t, d_ff // tf),
        in_specs=[tok, pl.BlockSpec((bb, 6, d), lambda bi, i, f: (bi, 0, 0)),
                  pl.BlockSpec((d, tf), lambda bi, i, f: (0, f)),
                  pl.BlockSpec((1, tf), lambda bi, i, f: (0, f)),
                  pl.BlockSpec((tf, d), lambda bi, i, f: (f, 0)), vec, vec],
        out_specs=tok,
        out_shape=jax.ShapeDtypeStruct(x.shape, F32),
        scratch_shapes=[pltpu.VMEM((bb * t, d), BF16)],
        compiler_params=_params(3, VMEM_LIMIT_BYTES),
        name="mlp",
    )(x, ada, w_up, b_up.reshape(1, d_ff), w_down, g.reshape(1, d), b.reshape(1, d))


def kernel(x_prompt, x_sample, cache_fox_k, cache_fox_v, cache_fox_logf, cache_band_k, cache_band_v,
           c_prompt, c_sample, w_ada, b_ada, w_in, b_forget, rel_bias, w_out, ln_mix_g, ln_mix_b,
           w_up, b_up, w_down, ln_mlp_g, ln_mlp_b):
    assert w_in.shape[0] == 1, "single-layer step"
    n_b, seq, d = x_prompt.shape
    n_bs, n_new, _ = x_sample.shape
    past = cache_fox_k.shape[2]
    gw = GROUP_WIDTH

    w = w_in[0]
    w_fox = w[:, :3 * gw].astype(BF16)
    w_band = w[:, 3 * gw + N_HEADS:].astype(BF16)
    wf_t = jnp.pad(w[:, 3 * gw:3 * gw + N_HEADS].T, ((0, 16 - N_HEADS), (0, 0))).astype(BF16)
    w_out_b = w_out[0].astype(BF16)
    w_up_b = w_up[0].astype(BF16)
    w_down_b = w_down[0].astype(BF16)

    ada = _ada(jnp.concatenate([c_prompt, c_sample], axis=0), w_ada[0], b_ada[0])
    ada = ada.reshape(n_b + n_bs, 6, d)
    ada_p, ada_s = ada[:n_b], ada[n_b:]
    bias = _band_bias(rel_bias[0], BAND_Q_BLOCK)

    qa, ka, va, qb, kb, vb, kb_tail, vb_tail, lft = _in_proj(
        x_prompt, ada_p, w_fox, w_band, wf_t, b_forget[0], bb=1, t=1024)
    f_p = _cumsum_lanes(lft)
    fox = _fox_prompt(qa, ka, va, f_p)
    band = _band_prompt(qb, kb, vb, bias)
    x1 = _out_proj(fox, band, x_prompt, ada_p, w_out_b, ln_mix_g[0], ln_mix_b[0], bb=1, t=1024)
    y_p = _mlp(x1, ada_p, w_up_b, b_up[0], w_down_b, ln_mlp_g[0], ln_mlp_b[0], bb=1, t=1024)

    heads = lambda a: a.reshape(1, a.shape[0], a.shape[1], N_HEADS, D_HEAD)
    p_states = (heads(ka), heads(va), jnp.swapaxes(lft, 1, 2)[None], heads(kb_tail), heads(vb_tail))

    qa_s, ka_s, va_s, qb_s, kb_s, vb_s, kb_tail_s, vb_tail_s, lft_s = _in_proj(
        x_sample, ada_s, w_fox, w_band, wf_t, b_forget[0], bb=n_bs, t=n_new)
    logf_s = jnp.transpose(lft_s.reshape(N_HEADS, n_bs, n_new), (1, 0, 2))
    total = past + n_new
    padded = -(-total // 128) * 128
    logf_all = jnp.concatenate(
        [jnp.swapaxes(cache_fox_logf[0], 1, 2), logf_s, jnp.zeros((n_bs, N_HEADS, padded - total), F32)], axis=2)
    f_s = _cumsum_lanes(logf_all)
    f_new = f_s[:, :, past:total]
    rows_view = lambda cache: cache.reshape(n_bs, cache.shape[2] * N_HEADS, D_HEAD)
    fox_s = _fox_sample(qa_s, rows_view(cache_fox_k), rows_view(cache_fox_v), ka_s, va_s,
                        f_s[:, :, :past], f_new, jnp.swapaxes(f_new, 1, 2))
    band_s = _band_sample(qb_s, rows_view(cache_band_k), rows_view(cache_band_v), kb_s, vb_s, bias)
    x1_s = _out_proj(fox_s, band_s, x_sample, ada_s, w_out_b, ln_mix_g[0], ln_mix_b[0], bb=n_bs, t=n_new)
    y_s = _mlp(x1_s, ada_s, w_up_b, b_up[0], w_down_b, ln_mlp_g[0], ln_mlp_b[0], bb=n_bs, t=n_new)
    s_states = (heads(ka_s), heads(va_s), jnp.swapaxes(logf_s, 1, 2)[None],
                heads(kb_tail_s), heads(vb_tail_s))

    return (y_p, y_s) + p_states + s_states
```

```python
import functools
import math

import jax
import jax.numpy as jnp
from jax import lax
from jax.experimental import pallas as pl
from jax.experimental.pallas import tpu as pltpu

D_HEAD = 128
N_HEADS = 8
GROUP_WIDTH = N_HEADS * D_HEAD
CHUNK = 64
BAND_ROWS = 512
BAND_Q_BLOCK = 256
MAX_REL = 128
N_REL = 2 * MAX_REL + 1
ALPHA = 2.0 ** 0.25
LN_EPS = 1e-5
NEG = -1e30
LOG2E = math.log2(math.e)
Q_SCALE = D_HEAD ** -0.5 * LOG2E

VMEM_LIMIT_BYTES = 60 * 1024 * 1024

F32 = jnp.float32
BF16 = jnp.bfloat16


def _params(n_axes, vmem=None):
    return pltpu.CompilerParams(dimension_semantics=("arbitrary",) * n_axes, vmem_limit_bytes=vmem)


def _norm(x):
    mu = jnp.mean(x, axis=-1, keepdims=True)
    xc = x - mu
    var = jnp.mean(xc * xc, axis=-1, keepdims=True)
    return xc * lax.rsqrt(var + LN_EPS)


def _dot_nt(a, b):
    return lax.dot_general(a, b, (((1,), (1,)), ((), ())), preferred_element_type=F32)


def _dot(a, b):
    return jnp.dot(a, b, preferred_element_type=F32)


ROW_CHUNK = 256


def _row_chunks(bb, t):
    if t >= ROW_CHUNK:
        return [(slice(b, b + 1), slice(r, r + ROW_CHUNK), slice(b * t + r, b * t + r + ROW_CHUNK))
                for b in range(bb) for r in range(0, t, ROW_CHUNK)]
    nb = min(ROW_CHUNK // t, bb)
    return [(slice(b, b + nb), slice(0, t), slice(b * t, (b + nb) * t)) for b in range(0, bb, nb)]


def _ada_kernel(c_ref, w_ref, b_ref, o_ref):
    a = jax.nn.silu(c_ref[...]).astype(BF16)
    o_ref[...] = _dot(a, w_ref[...].astype(BF16)) + b_ref[...]


def _ada(c, w_ada, b_ada, tn=1024):
    n_rows, d = c.shape
    n_out = w_ada.shape[1]
    return pl.pallas_call(
        _ada_kernel,
        grid=(n_out // tn,),
        in_specs=[pl.BlockSpec((n_rows, d), lambda j: (0, 0)),
                  pl.BlockSpec((d, tn), lambda j: (0, j)),
                  pl.BlockSpec((1, tn), lambda j: (0, j))],
        out_specs=pl.BlockSpec((n_rows, tn), lambda j: (0, j)),
        out_shape=jax.ShapeDtypeStruct((n_rows, n_out), F32),
        compiler_params=_params(1, 40 * 1024 * 1024),
        name="ada",
    )(c, w_ada, b_ada.reshape(1, n_out))


def _band_bias_kernel(tab_ref, o_ref):
    h = pl.program_id(0)
    t = BAND_ROWS
    w = 2 * t
    kidx = lax.broadcasted_iota(jnp.int32, (1, w), 1)
    d = jnp.where(kidx < t, kidx, kidx - w)
    idx_cur = jnp.clip(d, -MAX_REL, MAX_REL) + MAX_REL
    idx_prev = jnp.clip(d - t, -MAX_REL, MAX_REL) + MAX_REL

    def fill(r, rows):
        rc, rp = rows
        val = tab_ref[h, r] * LOG2E
        return jnp.where(idx_cur == r, val, rc), jnp.where(idx_prev == r, val, rp)

    rc, rp = lax.fori_loop(0, N_REL, fill, (jnp.zeros((1, w), F32), jnp.zeros((1, w), F32)))
    n_q = o_ref.shape[2]
    qrow = lax.broadcasted_iota(jnp.int32, (n_q, w), 0)

    def toeplitz(row):
        x = jnp.broadcast_to(row, (n_q, w))
        for bit in range(n_q.bit_length() - 1):
            x = jnp.where(((qrow >> bit) & 1) == 1, pltpu.roll(x, 1 << bit, 1), x)
        return x[:, :t]

    qc = lax.broadcasted_iota(jnp.int32, (n_q, t), 0) // CHUNK
    kc = lax.broadcasted_iota(jnp.int32, (n_q, t), 1) // CHUNK
    o_ref[0, 1] = jnp.where(kc <= qc, toeplitz(rc), NEG)
    o_ref[0, 0] = jnp.where(kc >= qc, toeplitz(rp), NEG)


def _band_bias(table, n_q):
    t = BAND_ROWS
    assert n_q & (n_q - 1) == 0 and n_q <= t
    return pl.pallas_call(
        _band_bias_kernel,
        grid=(N_HEADS,),
        in_specs=[pl.BlockSpec(memory_space=pltpu.SMEM)],
        out_specs=pl.BlockSpec((1, 2, n_q, t), lambda h: (h, 0, 0, 0)),
        out_shape=jax.ShapeDtypeStruct((N_HEADS, 2, n_q, t), F32),
        compiler_params=_params(1, 40 * 1024 * 1024),
        name="band_bias",
    )(table)


def _in_proj_kernel(x_ref, ada_ref, wfox_ref, wband_ref, wf_ref, bf_ref,
                    qa_ref, ka_ref, va_ref, qb_ref, kb_ref, vb_ref, kbt_ref, vbt_ref, lf_ref, u_ref,
                    *, tiles_per_out):
    j = pl.program_id(2)
    bb, t, d = x_ref.shape
    tn = wfox_ref.shape[1]
    tail = kbt_ref.shape[1]
    in_tail_block = pl.program_id(1) == pl.num_programs(1) - 1

    @pl.when(j == 0)
    def _():
        for bs, rs, flat in _row_chunks(bb, t):
            x = x_ref[bs, rs, :]
            u = _norm(x) * (1.0 + ada_ref[bs, 1:2, :]) + ada_ref[bs, 0:1, :]
            u = u.reshape(flat.stop - flat.start, d).astype(BF16)
            u_ref[flat, :] = u
            qa_ref[bs, rs, :] = (_dot(u, wfox_ref[...]) * Q_SCALE).astype(qa_ref.dtype).reshape(x.shape[:2] + (tn,))
        f = _dot_nt(wf_ref[...], u_ref[...])
        lf_ref[0] = jax.nn.log_sigmoid(f[:N_HEADS] + bf_ref[...])

    outs = ((qa_ref, Q_SCALE, None), (ka_ref, None, None), (va_ref, None, None),
            (qb_ref, Q_SCALE, None), (kb_ref, None, kbt_ref), (vb_ref, None, vbt_ref))
    for o, (ref, scale, tail_ref) in enumerate(outs):
        @pl.when(jnp.logical_and(j // tiles_per_out == o, j > 0))
        def _(ref=ref, scale=scale, tail_ref=tail_ref, w_ref=wfox_ref if o < 3 else wband_ref):
            acc = _dot(u_ref[...], w_ref[...])
            val = acc if scale is None else acc * scale
            ref[...] = val.astype(ref.dtype).reshape(bb, t, tn)
            if tail_ref is not None:
                @pl.when(in_tail_block)
                def _():
                    tail_ref[...] = acc.reshape(bb, t, tn)[:, t - tail:, :]


def _in_proj(x, ada, w_fox, w_band, wf_t, b_forget, *, bb, t, tn=512):
    n_b, s, d = x.shape
    m = bb * t
    tiles_per_out = GROUP_WIDTH // tn
    n_j = 6 * tiles_per_out

    n_i = s // t

    def out_spec(o):
        def index(b, i, j):
            return b, i, jnp.clip(j - o * tiles_per_out, 0, tiles_per_out - 1)
        return pl.BlockSpec((bb, t, tn), index)

    tail = min(BAND_ROWS, s)

    def tail_spec(o):
        def index(b, i, j):
            col = jnp.where(i == n_i - 1, jnp.clip(j - o * tiles_per_out, 0, tiles_per_out - 1), 0)
            return b, 0, col
        return pl.BlockSpec((bb, tail, tn), index)

    tok = lambda dt: jax.ShapeDtypeStruct((n_b, s, GROUP_WIDTH), dt)
    tail_shape = jax.ShapeDtypeStruct((n_b, tail, GROUP_WIDTH), F32)
    return pl.pallas_call(
        functools.partial(_in_proj_kernel, tiles_per_out=tiles_per_out),
        grid=(n_b // bb, s // t, n_j),
        in_specs=[pl.BlockSpec((bb, t, d), lambda b, i, j: (b, i, 0)),
                  pl.BlockSpec((bb, 6, d), lambda b, i, j: (b, 0, 0)),
                  pl.BlockSpec((d, tn), lambda b, i, j: (0, jnp.minimum(j, n_j // 2 - 1))),
                  pl.BlockSpec((d, tn), lambda b, i, j: (0, jnp.maximum(j - n_j // 2, 0))),
                  pl.BlockSpec(wf_t.shape, lambda b, i, j: (0, 0)),
                  pl.BlockSpec((N_HEADS, 1), lambda b, i, j: (0, 0))],
        out_specs=[out_spec(o) for o in range(6)] + [tail_spec(4), tail_spec(5)]
        + [pl.BlockSpec((1, N_HEADS, m), lambda b, i, j: (b, 0, i))],
        out_shape=[tok(BF16), tok(F32), tok(F32), tok(BF16), tok(BF16), tok(BF16), tail_shape, tail_shape,
                   jax.ShapeDtypeStruct((n_b // bb, N_HEADS, (s // t) * m), F32)],
        scratch_shapes=[pltpu.VMEM((m, d), BF16)],
        compiler_params=_params(3, VMEM_LIMIT_BYTES),
        name="in_proj",
    )(x, ada, w_fox, w_band, wf_t, b_forget.reshape(N_HEADS, 1))


def _cumsum_kernel(x_ref, o_ref):
    x = x_ref[0]
    n = x.shape[1]
    lane = lax.broadcasted_iota(jnp.int32, x.shape, 1)
    shift = 1
    while shift < n:
        x = x + jnp.where(lane >= shift, pltpu.roll(x, shift, 1), 0.0)
        shift *= 2
    o_ref[0] = x * LOG2E


def _cumsum_lanes(x):
    n_b, h, n = x.shape
    spec = pl.BlockSpec((1, h, n), lambda b: (b, 0, 0))
    return pl.pallas_call(
        _cumsum_kernel, grid=(n_b,), in_specs=[spec], out_specs=spec,
        out_shape=jax.ShapeDtypeStruct(x.shape, F32),
        compiler_params=_params(1), name="cumsum",
    )(x)


def _fox_prompt_kernel(q_ref, k_ref, v_ref, frow_ref, o_ref,
                       kb_ref, vb_ref, fq_ref, m_ref, l_ref, acc_ref, *, blk, strip, hp):
    qi = pl.program_id(2)
    lanes = D_HEAD
    n_col = blk // lanes
    head_lanes = [slice(hh * lanes, (hh + 1) * lanes) for hh in range(hp)]

    @pl.when(qi == 0)
    def _():
        for hh in range(hp):
            kb_ref[hh] = k_ref[0, :, head_lanes[hh]].astype(BF16)
            vb_ref[hh] = v_ref[0, :, head_lanes[hh]].astype(BF16)

    on_diag = (lax.broadcasted_iota(jnp.int32, (blk, blk), 0) == lax.broadcasted_iota(jnp.int32, (blk, blk), 1))
    for hh in range(hp):
        f_col = jnp.sum(jnp.where(on_diag, frow_ref[hh, qi], 0.0), axis=1, keepdims=True)
        fq_ref[hh] = jnp.broadcast_to(f_col, (blk, lanes))
    m_ref[...] = jnp.full(m_ref.shape, NEG, F32)
    l_ref[...] = jnp.zeros(l_ref.shape, F32)
    acc_ref[...] = jnp.zeros(acc_ref.shape, F32)

    def scores(hh, j):
        start = pl.multiple_of(j * blk, blk)
        return _dot_nt(q_ref[0, :, head_lanes[hh]], kb_ref[hh, pl.ds(start, blk), :])

    def softmax(hh, j, s_all, masked):
        f_keys = frow_ref[hh, j]
        p_strips = []
        for r in range(blk // strip):
            rows = slice(r * strip, (r + 1) * strip)
            fq = fq_ref[hh, rows]
            cols = []
            for cc in range(n_col):
                k0 = cc * lanes
                if masked and k0 >= (r + 1) * strip:
                    continue
                s = s_all[rows, k0:k0 + lanes] + fq - f_keys[:, k0:k0 + lanes]
                if masked and k0 + lanes - 1 > r * strip:
                    row = lax.broadcasted_iota(jnp.int32, s.shape, 0) + r * strip
                    col = lax.broadcasted_iota(jnp.int32, s.shape, 1) + k0
                    s = jnp.where(col <= row, s, NEG)
                cols.append(s)
            m_old = m_ref[hh, rows]
            m_blk = functools.reduce(jnp.maximum, cols)
            m_new = jnp.maximum(m_old, jnp.broadcast_to(jnp.max(m_blk, axis=1, keepdims=True), m_old.shape))
            ps = [jnp.exp2(s - m_new) for s in cols]
            a = jnp.exp2(m_old - m_new)
            l_ref[hh, rows] = a * l_ref[hh, rows] + functools.reduce(jnp.add, ps)
            m_ref[hh, rows] = m_new
            acc_ref[hh, rows] = a * acc_ref[hh, rows]
            ps = [p.astype(BF16) for p in ps] + [jnp.zeros((strip, lanes), BF16)] * (n_col - len(ps))
            p_strips.append(jnp.concatenate(ps, axis=1))
        return jnp.concatenate(p_strips, axis=0)

    def values(hh, j, p):
        start = pl.multiple_of(j * blk, blk)
        acc_ref[hh] += _dot(p, vb_ref[hh, pl.ds(start, blk), :])

    def block_steps(blocks):
        chains = [(hh, j, masked) for j, masked in blocks for hh in range(hp)]
        s_next = scores(*chains[0][:2])
        for n, (hh, j, masked) in enumerate(chains):
            s_cur = s_next
            if n + 1 < len(chains):
                s_next = scores(*chains[n + 1][:2])
            values(hh, j, softmax(hh, j, s_cur, masked))

    def pair_body(p, carry):
        block_steps([(2 * p, False), (2 * p + 1, False)])
        return carry

    lax.fori_loop(0, lax.shift_right_logical(qi, 1), pair_body, 0)

    @pl.when((qi & 1) == 1)
    def _():
        block_steps([(qi - 1, False)])

    block_steps([(qi, True)])
    for hh in range(hp):
        l = jnp.sum(l_ref[hh], axis=1, keepdims=True)
        o_ref[0, :, head_lanes[hh]] = (acc_ref[hh] / l).astype(o_ref.dtype)


def _fox_prompt(q, k, v, f, *, blk=512, strip=32, hp=4):
    n_b, s, _ = q.shape
    n_blk = s // blk
    n_hg = N_HEADS // hp
    frow = f.reshape(n_b * N_HEADS, n_blk, 1, blk)
    kv_spec = pl.BlockSpec((1, s, hp * D_HEAD), lambda b, h, i: (b, 0, h))
    q_spec = pl.BlockSpec((1, blk, hp * D_HEAD), lambda b, h, i: (b, i, h))
    state = pltpu.VMEM((hp, blk, D_HEAD), F32)
    return pl.pallas_call(
        functools.partial(_fox_prompt_kernel, blk=blk, strip=strip, hp=hp),
        grid=(n_b, n_hg, n_blk),
        in_specs=[q_spec, kv_spec, kv_spec,
                  pl.BlockSpec((hp, n_blk, 1, blk), lambda b, h, i: (b * n_hg + h, 0, 0, 0))],
        out_specs=q_spec,
        out_shape=jax.ShapeDtypeStruct((n_b, s, GROUP_WIDTH), BF16),
        scratch_shapes=[pltpu.VMEM((hp, s, D_HEAD), BF16), pltpu.VMEM((hp, s, D_HEAD), BF16),
                        state, state, state, state],
        compiler_params=_params(3, VMEM_LIMIT_BYTES),
        name="fox_prompt",
    )(q, k, v, frow)


def _band_prompt_kernel(q_ref, k_ref, v_ref, bias_ref, o_ref, kb_ref, vb_ref, *, tb, nq, strip, hp):
    i = pl.program_id(2)
    past = BAND_ROWS
    lanes = D_HEAD
    win = past + tb
    head_lanes = [slice(hh * lanes, (hh + 1) * lanes) for hh in range(hp)]

    @pl.when(i == 0)
    def _():
        for hh in range(hp):
            kb_ref[hh, :past, :] = jnp.zeros((past, D_HEAD), BF16)
            vb_ref[hh, :past, :] = jnp.zeros((past, D_HEAD), BF16)
            kb_ref[hh, past:, :] = k_ref[0, :, head_lanes[hh]]
            vb_ref[hh, past:, :] = v_ref[0, :, head_lanes[hh]]

    chains = [(hh, qq) for qq in range(nq) for hh in range(hp)]

    def block_start(qq):
        return pl.multiple_of((i * nq + qq) * tb, tb)

    def scores(chain):
        hh, qq = chain
        q = q_ref[0, qq * tb:(qq + 1) * tb, head_lanes[hh]]
        return _dot_nt(q, kb_ref[hh, pl.ds(block_start(qq), win), :])

    def softmax(chain, s_all):
        hh, qq = chain
        first_valid = past - block_start(qq)
        p_strips, l_strips = [], []
        for r in range(tb // strip):
            rows = slice(r * strip, (r + 1) * strip)
            q_chunk = (r * strip) // CHUNK
            cols = {}
            for cc in range(win // lanes):
                k0 = cc * lanes
                if k0 < past:
                    if (k0 + lanes - 1) // CHUNK < q_chunk:
                        continue
                    s = s_all[rows, k0:k0 + lanes] + bias_ref[hh, 0, rows, k0:k0 + lanes]
                    col = lax.broadcasted_iota(jnp.int32, s.shape, 1) + k0
                    s = jnp.where(col >= first_valid, s, NEG)
                else:
                    if (k0 - past) // CHUNK > q_chunk:
                        continue
                    s = s_all[rows, k0:k0 + lanes] + bias_ref[hh, 1, rows, k0 - past:k0 - past + lanes]
                cols[cc] = s
            m_blk = functools.reduce(jnp.maximum, cols.values())
            m = jnp.broadcast_to(jnp.max(m_blk, axis=1, keepdims=True), m_blk.shape)
            ps = {cc: jnp.exp2(s - m) for cc, s in cols.items()}
            l_strips.append(functools.reduce(jnp.add, ps.values()))
            zero = jnp.zeros((strip, lanes), BF16)
            p_strips.append(jnp.concatenate(
                [ps[cc].astype(BF16) if cc in ps else zero for cc in range(win // lanes)], axis=1))
        return jnp.concatenate(p_strips, axis=0), jnp.concatenate(l_strips, axis=0)

    def values(chain, p, l_part):
        hh, qq = chain
        acc = _dot(p, vb_ref[hh, pl.ds(block_start(qq), win), :])
        l = jnp.sum(l_part, axis=1, keepdims=True)
        o_ref[0, qq * tb:(qq + 1) * tb, head_lanes[hh]] = (acc / l).astype(o_ref.dtype)

    s_next = scores(chains[0])
    for n, chain in enumerate(chains):
        s_cur = s_next
        if n + 1 < len(chains):
            s_next = scores(chains[n + 1])
        values(chain, *softmax(chain, s_cur))


def _band_prompt(q, k, v, bias, *, tb=BAND_Q_BLOCK, nq=8, strip=32, hp=4):
    n_b, s, _ = q.shape
    kv_spec = pl.BlockSpec((1, s, hp * D_HEAD), lambda h, b, i: (b, 0, h))
    q_spec = pl.BlockSpec((1, nq * tb, hp * D_HEAD), lambda h, b, i: (b, i, h))
    kv_scratch = pltpu.VMEM((hp, BAND_ROWS + s, D_HEAD), BF16)
    return pl.pallas_call(
        functools.partial(_band_prompt_kernel, tb=tb, nq=nq, strip=strip, hp=hp),
        grid=(N_HEADS // hp, n_b, s // (nq * tb)),
        in_specs=[q_spec, kv_spec, kv_spec,
                  pl.BlockSpec((hp, 2, tb, BAND_ROWS), lambda h, b, i: (h, 0, 0, 0))],
        out_specs=q_spec,
        out_shape=jax.ShapeDtypeStruct((n_b, s, GROUP_WIDTH), BF16),
        scratch_shapes=[kv_scratch, kv_scratch],
        compiler_params=_params(3, VMEM_LIMIT_BYTES),
        name="band_prompt",
    )(q, k, v, bias)


def _fox_sample_kernel(q_ref, kc_ref, vc_ref, kn_ref, vn_ref, fc_ref, fn_ref, fq_ref, o_ref,
                       m_ref, l_ref, acc_ref, *, tc):
    c = pl.program_id(1)
    n_new = q_ref.shape[1]
    lanes = D_HEAD

    @pl.when(c == 0)
    def _():
        m_ref[...] = jnp.full(m_ref.shape, NEG, F32)
        l_ref[...] = jnp.zeros(l_ref.shape, F32)
        acc_ref[...] = jnp.zeros(acc_ref.shape, F32)

    def update(h, s, v):
        m_old = m_ref[h][:, :1]
        m_new = jnp.maximum(m_old, jnp.max(s, axis=1, keepdims=True))
        p = jnp.exp2(s - m_new)
        a = jnp.exp2(m_old - m_new)
        l_ref[h] = a * l_ref[h] + jnp.sum(p, axis=1, keepdims=True)
        m_ref[h] = jnp.broadcast_to(m_new, (n_new, lanes))
        acc_ref[h] = a * acc_ref[h] + _dot(p.astype(BF16), v)

    for h in range(N_HEADS):
        hl = slice(h * lanes, (h + 1) * lanes)
        rows = pl.ds(h, tc, stride=N_HEADS)
        s = (_dot_nt(q_ref[0, :, hl], kc_ref[0, rows, :].astype(BF16))
             + fq_ref[0, :, h:h + 1] - fc_ref[0, 0, h:h + 1, :])
        update(h, s, vc_ref[0, rows, :].astype(BF16))

    @pl.when(c == pl.num_programs(1) - 1)
    def _():
        for h in range(N_HEADS):
            hl = slice(h * lanes, (h + 1) * lanes)
            s = (_dot_nt(q_ref[0, :, hl], kn_ref[0, :, hl].astype(BF16))
                 + fq_ref[0, :, h:h + 1] - fn_ref[0, h:h + 1, :])
            row = lax.broadcasted_iota(jnp.int32, s.shape, 0)
            col = lax.broadcasted_iota(jnp.int32, s.shape, 1)
            update(h, jnp.where(col <= row, s, NEG), vn_ref[0, :, hl].astype(BF16))
            o_ref[0, :, hl] = (acc_ref[h] / l_ref[h]).astype(o_ref.dtype)


def _fox_sample(q, k_cache, v_cache, k_new, v_new, f_cache, f_new, f_q, *, tc=2048):
    n_b, n_new, _ = q.shape
    past = k_cache.shape[1] // N_HEADS
    n_chunks = past // tc
    f_cache = jnp.swapaxes(f_cache.reshape(n_b, N_HEADS, n_chunks, tc), 1, 2)
    new_spec = pl.BlockSpec((1, n_new, GROUP_WIDTH), lambda b, c: (b, 0, 0))
    cache_spec = pl.BlockSpec((1, tc * N_HEADS, D_HEAD), lambda b, c: (b, c, 0))
    state = pltpu.VMEM((N_HEADS, n_new, D_HEAD), F32)
    return pl.pallas_call(
        functools.partial(_fox_sample_kernel, tc=tc),
        grid=(n_b, n_chunks),
        in_specs=[new_spec, cache_spec, cache_spec, new_spec, new_spec,
                  pl.BlockSpec((1, 1, N_HEADS, tc), lambda b, c: (b, c, 0, 0)),
                  pl.BlockSpec((1, N_HEADS, n_new), lambda b, c: (b, 0, 0)),
                  pl.BlockSpec((1, n_new, N_HEADS), lambda b, c: (b, 0, 0))],
        out_specs=new_spec,
        out_shape=jax.ShapeDtypeStruct((n_b, n_new, GROUP_WIDTH), BF16),
        scratch_shapes=[state, state, state],
        compiler_params=_params(2, 40 * 1024 * 1024),
        name="fox_sample",
    )(q, k_cache, v_cache, k_new, v_new, f_cache, f_new, f_q)


def _band_sample_kernel(q_ref, kc_ref, vc_ref, kn_ref, vn_ref, bias_ref, o_ref):
    n_new = q_ref.shape[1]
    past = kc_ref.shape[1] // N_HEADS
    lanes = D_HEAD
    for h in range(N_HEADS):
        hl = slice(h * lanes, (h + 1) * lanes)
        rows = pl.ds(h, past, stride=N_HEADS)
        q = q_ref[0, :, hl]
        s_c = _dot_nt(q, kc_ref[0, rows, :].astype(BF16)) + bias_ref[h, 0]
        s_n = _dot_nt(q, kn_ref[0, :, hl].astype(BF16)) + bias_ref[h, 1][:, :n_new]
        m = jnp.maximum(jnp.max(s_c, axis=1, keepdims=True), jnp.max(s_n, axis=1, keepdims=True))
        p_c = jnp.exp2(s_c - m)
        p_n = jnp.exp2(s_n - m)
        l = jnp.sum(p_c, axis=1, keepdims=True) + jnp.sum(p_n, axis=1, keepdims=True)
        acc = (_dot(p_c.astype(BF16), vc_ref[0, rows, :].astype(BF16))
               + _dot(p_n.astype(BF16), vn_ref[0, :, hl].astype(BF16)))
        o_ref[0, :, hl] = (acc / l).astype(o_ref.dtype)


def _band_sample(q, k_cache, v_cache, k_new, v_new, bias):
    n_b, n_new, _ = q.shape
    new_spec = pl.BlockSpec((1, n_new, GROUP_WIDTH), lambda b: (b, 0, 0))
    cache_spec = pl.BlockSpec((1,) + k_cache.shape[1:], lambda b: (b, 0, 0))
    return pl.pallas_call(
        _band_sample_kernel,
        grid=(n_b,),
        in_specs=[new_spec, cache_spec, cache_spec, new_spec, new_spec,
                  pl.BlockSpec((N_HEADS, 2, n_new, BAND_ROWS), lambda b: (0, 0, 0, 0))],
        out_specs=new_spec,
        out_shape=jax.ShapeDtypeStruct((n_b, n_new, GROUP_WIDTH), BF16),
        compiler_params=_params(1, 40 * 1024 * 1024),
        name="band_sample",
    )(q, k_cache, v_cache, k_new, v_new, bias)


def _out_proj_kernel(fox_ref, band_ref, x_ref, ada_ref, w_ref, g_ref, b_ref, o_ref):
    bb, t, d = x_ref.shape
    gw = fox_ref.shape[2]
    for bs, rs, flat in _row_chunks(bb, t):
        x = x_ref[bs, rs, :]
        rows = flat.stop - flat.start
        y = (_dot(fox_ref[bs, rs, :].reshape(rows, gw), w_ref[:gw, :])
             + _dot(band_ref[bs, rs, :].reshape(rows, gw), w_ref[gw:, :]))
        z = ALPHA * x + ada_ref[bs, 2:3, :] * y.reshape(x.shape)
        o_ref[bs, rs, :] = _norm(z) * g_ref[...] + b_ref[...]


def _out_proj(fox, band, x, ada, w_out, g, b, *, bb, t):
    n_b, s, d = x.shape
    tok = lambda w: pl.BlockSpec((bb, t, w), lambda bi, i: (bi, i, 0))
    vec = pl.BlockSpec((1, d), lambda bi, i: (0, 0))
    return pl.pallas_call(
        _out_proj_kernel,
        grid=(n_b // bb, s // t),
        in_specs=[tok(GROUP_WIDTH), tok(GROUP_WIDTH), tok(d),
                  pl.BlockSpec((bb, 6, d), lambda bi, i: (bi, 0, 0)),
                  pl.BlockSpec(w_out.shape, lambda bi, i: (0, 0)), vec, vec],
        out_specs=tok(d),
        out_shape=jax.ShapeDtypeStruct(x.shape, F32),
        compiler_params=_params(2, VMEM_LIMIT_BYTES),
        name="out_proj",
    )(fox, band, x, ada, w_out, g.reshape(1, d), b.reshape(1, d))


def _mlp_kernel(x_ref, ada_ref, wu_ref, bu_ref, wd_ref, g_ref, b_ref, o_ref, u_ref):
    f = pl.program_id(2)
    last = pl.num_programs(2) - 1
    bb, t, d = x_ref.shape
    chunks = _row_chunks(bb, t)

    def hidden(u):
        hid = jnp.maximum(_dot(u, wu_ref[...]) + bu_ref[...], 0.0)
        return (hid * hid).astype(BF16)

    @pl.when(f == 0)
    def _():
        for bs, rs, flat in chunks:
            u = _norm(x_ref[bs, rs, :]) * (1.0 + ada_ref[bs, 4:5, :]) + ada_ref[bs, 3:4, :]
            u = u.reshape(flat.stop - flat.start, d).astype(BF16)
            u_ref[flat, :] = u
            o_ref[bs, rs, :] = _dot(hidden(u), wd_ref[...]).reshape(x_ref[bs, rs, :].shape)

    @pl.when(jnp.logical_and(f > 0, f < last))
    def _():
        o_ref[...] += _dot(hidden(u_ref[...]), wd_ref[...]).reshape(bb, t, d)

    @pl.when(f == last)
    def _():
        for bs, rs, flat in chunks:
            x = x_ref[bs, rs, :]
            y = o_ref[bs, rs, :] + _dot(hidden(u_ref[flat, :]), wd_ref[...]).reshape(x.shape)
            z = ALPHA * x + ada_ref[bs, 5:6, :] * y
            o_ref[bs, rs, :] = _norm(z) * g_ref[...] + b_ref[...]


def _mlp(x, ada, w_up, b_up, w_down, g, b, *, bb, t, tf=1024):
    n_b, s, d = x.shape
    d_ff = w_up.shape[1]
    assert d_ff // tf >= 2, "first and last hidden tiles are separate steps"
    tok = pl.BlockSpec((bb, t, d), lambda bi, i, f: (bi, i, 0))
    vec = pl.BlockSpec((1, d), lambda bi, i, f: (0, 0))
    return pl.pallas_call(
        _mlp_kernel,
        grid=(n_b // bb, s // t, d_ff // tf),
        in_specs=[tok, pl.BlockSpec((bb, 6, d), lambda bi, i, f: (bi, 0, 0)),
                  pl.BlockSpec((d, tf), lambda bi, i, f: (0, f)),
                  pl.BlockSpec((1, tf), lambda bi, i, f: (0, f)),
                  pl.BlockSpec((tf, d), lambda bi, i, f: (f, 0)), vec, vec],
        out_specs=tok,
        out_shape=jax.ShapeDtypeStruct(x.shape, F32),
        scratch_shapes=[pltpu.VMEM((bb * t, d), BF16)],
        compiler_params=_params(3, VMEM_LIMIT_BYTES),
        name="mlp",
    )(x, ada, w_up, b_up.reshape(1, d_ff), w_down, g.reshape(1, d), b.reshape(1, d))


def kernel(x_prompt, x_sample, cache_fox_k, cache_fox_v, cache_fox_logf, cache_band_k, cache_band_v,
           c_prompt, c_sample, w_ada, b_ada, w_in, b_forget, rel_bias, w_out, ln_mix_g, ln_mix_b,
           w_up, b_up, w_down, ln_mlp_g, ln_mlp_b):
    assert w_in.shape[0] == 1, "single-layer step"
    n_b, seq, d = x_prompt.shape
    n_bs, n_new, _ = x_sample.shape
    past = cache_fox_k.shape[2]
    gw = GROUP_WIDTH

    w = w_in[0]
    w_fox = w[:, :3 * gw].astype(BF16)
    w_band = w[:, 3 * gw + N_HEADS:].astype(BF16)
    wf_t = jnp.pad(w[:, 3 * gw:3 * gw + N_HEADS].T, ((0, 16 - N_HEADS), (0, 0))).astype(BF16)
    w_out_b = w_out[0].astype(BF16)
    w_up_b = w_up[0].astype(BF16)
    w_down_b = w_down[0].astype(BF16)

    ada = _ada(jnp.concatenate([c_prompt, c_sample], axis=0), w_ada[0], b_ada[0])
    ada = ada.reshape(n_b + n_bs, 6, d)
    ada_p, ada_s = ada[:n_b], ada[n_b:]
    bias = _band_bias(rel_bias[0], BAND_Q_BLOCK)

    qa, ka, va, qb, kb, vb, kb_tail, vb_tail, lft = _in_proj(
        x_prompt, ada_p, w_fox, w_band, wf_t, b_forget[0], bb=1, t=1024)
    f_p = _cumsum_lanes(lft)
    fox = _fox_prompt(qa, ka, va, f_p)
    band = _band_prompt(qb, kb, vb, bias)
    x1 = _out_proj(fox, band, x_prompt, ada_p, w_out_b, ln_mix_g[0], ln_mix_b[0], bb=1, t=1024)
    y_p = _mlp(x1, ada_p, w_up_b, b_up[0], w_down_b, ln_mlp_g[0], ln_mlp_b[0], bb=1, t=1024)

    heads = lambda a: a.reshape(1, a.shape[0], a.shape[1], N_HEADS, D_HEAD)
    p_states = (heads(ka), heads(va), jnp.swapaxes(lft, 1, 2)[None], heads(kb_tail), heads(vb_tail))

    qa_s, ka_s, va_s, qb_s, kb_s, vb_s, kb_tail_s, vb_tail_s, lft_s = _in_proj(
        x_sample, ada_s, w_fox, w_band, wf_t, b_forget[0], bb=n_bs, t=n_new)
    logf_s = jnp.transpose(lft_s.reshape(N_HEADS, n_bs, n_new), (1, 0, 2))
    total = past + n_new
    padded = -(-total // 128) * 128
    logf_all = jnp.concatenate(
        [jnp.swapaxes(cache_fox_logf[0], 1, 2), logf_s, jnp.zeros((n_bs, N_HEADS, padded - total), F32)], axis=2)
    f_s = _cumsum_lanes(logf_all)
    f_new = f_s[:, :, past:total]
    rows_view = lambda cache: cache.reshape(n_bs, cache.shape[2] * N_HEADS, D_HEAD)
    fox_s = _fox_sample(qa_s, rows_view(cache_fox_k), rows_view(cache_fox_v), ka_s, va_s,
                        f_s[:, :, :past], f_new, jnp.swapaxes(f_new, 1, 2))
    band_s = _band_sample(qb_s, rows_view(cache_band_k), rows_view(cache_band_v), kb_s, vb_s, bias)
    x1_s = _out_proj(fox_s, band_s, x_sample, ada_s, w_out_b, ln_mix_g[0], ln_mix_b[0], bb=n_bs, t=n_new)
    y_s = _mlp(x1_s, ada_s, w_up_b, b_up[0], w_down_b, ln_mlp_g[0], ln_mlp_b[0], bb=n_bs, t=n_new)
    s_states = (heads(ka_s), heads(va_s), jnp.swapaxes(logf_s, 1, 2)[None],
                heads(kb_tail_s), heads(vb_tail_s))

    return (y_p, y_s) + p_states + s_states
```
